```python
import math
import jax, jax.numpy as jnp
from jax import lax
import numpy as np

D_MODEL = 4096
BATCH = 4
SEQ = 2048
DEPTH = 1
DEC_BATCH = 32
DEC_SEQ = 1
PAST_LEN = 8192
PAGE_SIZE = 128

MIX_WIDTH = D_MODEL
HG_WIDTH = MIX_WIDTH // 2
MB_WIDTH = MIX_WIDTH - HG_WIDTH
HG_HEADS = 16
HG_KDIM = HG_WIDTH // HG_HEADS
HG_VDIM = HG_WIDTH // HG_HEADS
MB_HEADS = 16
MB_HDIM = MB_WIDTH // MB_HEADS
MB_BLOCK = 256
MB_TOPK = 3
MB_Q_CHUNK = 32
GLA_CHUNK = 64
D_FF = 11008
CONV_W = 3
PLE_DIM = 256
RMS_EPS = 1e-6
IN_COLS = 4 * HG_WIDTH + 3 * MB_WIDTH

kernel_name = "hymba_hgrn2_moba_convffn_step"


def rmsnorm(x, g):
    xf = x.astype(jnp.float32)
    y = xf * lax.rsqrt(jnp.mean(xf * xf, axis=-1, keepdims=True) + RMS_EPS)
    return (y * g.astype(jnp.float32)).astype(x.dtype)


def alibi_slopes():
    return 2.0 ** (-8.0 * jnp.arange(1, MB_HEADS + 1, dtype=jnp.float32) / MB_HEADS)


def gla_chunked(q, k, v, logf, s0):
    B, L, H, K = q.shape
    V = v.shape[-1]
    C = GLA_CHUNK if L % GLA_CHUNK == 0 else L
    n = L // C

    def to_chunks(a):
        return a.astype(jnp.float32).reshape(B, n, C, H, a.shape[-1]).transpose(1, 0, 3, 2, 4)

    causal = jnp.tril(jnp.ones((C, C), dtype=bool))

    def step(S, inp):
        qi, ki, vi, gi = inp
        cum = jnp.cumsum(gi, axis=2)
        q_dec = qi * jnp.exp(cum)
        k_inv = ki * jnp.exp(-cum)
        att = jnp.where(causal, jnp.einsum('bhtk,bhsk->bhts', q_dec, k_inv), 0.0)
        o = jnp.einsum('bhtk,bhkv->bhtv', q_dec, S) + jnp.einsum('bhts,bhsv->bhtv', att, vi)
        last = cum[:, :, -1:, :]
        S = jnp.exp(last[:, :, 0, :, None]) * S + jnp.einsum('bhsk,bhsv->bhkv', ki * jnp.exp(last - cum), vi)
        return S, o

    S, o = lax.scan(step, s0.astype(jnp.float32), (to_chunks(q), to_chunks(k), to_chunks(v), to_chunks(logf)))
    o = o.transpose(1, 0, 3, 2, 4).reshape(B, L, H, V)
    return o, S.astype(s0.dtype)


def hgrn2(zq, zf, zi, zg, lb, b_f, g_onorm, s0):
    B, L, _ = zq.shape
    zf = zf.astype(jnp.float32) + b_f.astype(jnp.float32)
    lb = lb.astype(jnp.float32)
    logf = jnp.log(lb + (1.0 - lb) * jax.nn.sigmoid(zf))
    key = (1.0 - lb) * jax.nn.sigmoid(-zf)

    def heads(a, d):
        return a.astype(jnp.float32).reshape(B, L, HG_HEADS, d)

    o, s_new = gla_chunked(heads(zq, HG_KDIM), heads(key, HG_KDIM), heads(zi, HG_VDIM),
                           heads(logf, HG_KDIM), s0)
    o = rmsnorm(o, g_onorm) * jax.nn.silu(heads(zg, HG_VDIM))
    return o.reshape(B, L, HG_WIDTH).astype(zq.dtype), s_new


def select_blocks(q, kmean, pos):
    nb = kmean.shape[1]
    score = jnp.einsum('blhd,bjhd->blhj', q.astype(jnp.float32), kmean)
    full_past = (jnp.arange(nb) + 1) * MB_BLOCK <= pos[:, None]
    score = jnp.where(full_past[None, :, None, :], score, -jnp.inf)
    vals, idx = lax.top_k(score, min(MB_TOPK, nb))
    ok = jnp.isfinite(vals)
    B, L, H, _ = score.shape
    own = jnp.broadcast_to((pos // MB_BLOCK)[None, :, None, None], (B, L, H, 1)).astype(idx.dtype)
    idx = jnp.concatenate([idx, own], axis=-1)
    ok = jnp.concatenate([ok, jnp.ones((B, L, H, 1), dtype=bool)], axis=-1)
    return idx, ok


def moba_attend(q, pos, blk_idx, blk_ok, fetch, slopes):
    rows = blk_idx[..., None] * MB_BLOCK + jnp.arange(MB_BLOCK)
    kk, vv = fetch(rows)
    s = jnp.einsum('qhd,qhjrd->qhjr', q.astype(jnp.float32), kk.astype(jnp.float32)) * (MB_HDIM ** -0.5)
    dist = (pos[:, None, None, None] - rows).astype(jnp.float32)
    s = s - slopes[None, :, None, None] * dist
    ok = blk_ok[..., None] & (dist >= 0)
    s = jnp.where(ok, s, -jnp.inf)
    Q, H, J, R = s.shape
    p = jax.nn.softmax(s.reshape(Q, H, J * R), axis=-1).reshape(Q, H, J, R)
    o = jnp.einsum('qhjr,qhjrd->qhd', p, vv.astype(jnp.float32))
    return o.astype(q.dtype)


def moba_prompt(q, k, v, slopes):
    B, L, H, D = q.shape
    nb = -(-L // MB_BLOCK)
    pos = jnp.arange(L)
    kpad = jnp.pad(k.astype(jnp.float32), ((0, 0), (0, nb * MB_BLOCK - L), (0, 0), (0, 0)))
    kmean = kpad.reshape(B, nb, MB_BLOCK, H, D).sum(axis=2) / MB_BLOCK
    idx, ok = select_blocks(q, kmean, pos)
    qc = MB_Q_CHUNK if L % MB_Q_CHUNK == 0 else L
    n = L // qc

    def chunks(a):
        return a.reshape((B * n, qc) + a.shape[2:])

    pos_c = jnp.broadcast_to(pos.reshape(1, n, qc), (B, n, qc)).reshape(B * n, qc)
    b_ids = jnp.repeat(jnp.arange(B), n)
    h_ar = jnp.arange(H)[None, :, None, None]

    def body(args):
        q_i, idx_i, ok_i, pos_i, b = args

        def fetch(rows):
            r = jnp.clip(rows, 0, L - 1)
            return k[b, r, h_ar], v[b, r, h_ar]

        return moba_attend(q_i, pos_i, idx_i, ok_i, fetch, slopes)

    o = lax.map(body, (chunks(q), chunks(idx), chunks(ok), pos_c, b_ids))
    return o.reshape(B, L, H, D)


def moba_sample(q, k, v, cache_k, cache_v, page_table, slopes):
    Bd, L, H, D = q.shape
    n_pages = page_table.shape[1]
    T = PAST_LEN + L
    nb = -(-T // MB_BLOCK)
    pos = PAST_LEN + jnp.arange(L)
    page_sum = cache_k.astype(jnp.float32).sum(axis=1)[page_table]
    page_blk = (jnp.arange(n_pages) * PAGE_SIZE) // MB_BLOCK
    sums = jax.ops.segment_sum(page_sum.transpose(1, 0, 2, 3), page_blk, num_segments=nb)
    sums = sums + jax.ops.segment_sum(k.astype(jnp.float32).transpose(1, 0, 2, 3), pos // MB_BLOCK,
                                      num_segments=nb)
    kmean = sums.transpose(1, 0, 2, 3) / MB_BLOCK
    idx, ok = select_blocks(q, kmean, pos)
    h_ar = jnp.arange(H)[None, :, None, None]

    def per_seq(q_b, idx_b, ok_b, k_b, v_b, pt_b):
        def fetch(rows):
            past = (rows < PAST_LEN)[..., None]
            rp = jnp.clip(rows, 0, PAST_LEN - 1)
            page = pt_b[rp // PAGE_SIZE]
            off = rp % PAGE_SIZE
            rn = jnp.clip(rows - PAST_LEN, 0, L - 1)
            kk = jnp.where(past, cache_k[page, off, h_ar], k_b[rn, h_ar].astype(cache_k.dtype))
            vv = jnp.where(past, cache_v[page, off, h_ar], v_b[rn, h_ar].astype(cache_v.dtype))
            return kk, vv

        return moba_attend(q_b, pos, idx_b, ok_b, fetch, slopes)

    return jax.vmap(per_seq)(q, idx, ok, k, v, page_table)


def block_forward(h, pe, s0, conv_prev, moba_fn, lb, w_in, b_f, g_onorm, w_out, g_n1, g_n2,
                  w_up, w_conv, b_conv, w_down, w_ple, w_pg, g_pn):
    B, L, _ = h.shape
    z = rmsnorm(h, g_n1) @ w_in
    cuts = [HG_WIDTH, 2 * HG_WIDTH, 3 * HG_WIDTH, 4 * HG_WIDTH,
            4 * HG_WIDTH + MB_WIDTH, 4 * HG_WIDTH + 2 * MB_WIDTH]
    zq, zf, zi, zg, mq, mk, mv = jnp.split(z, cuts, axis=-1)
    o_hg, s_new = hgrn2(zq, zf, zi, zg, lb, b_f, g_onorm, s0)
    mq = mq.reshape(B, L, MB_HEADS, MB_HDIM)
    mk = mk.reshape(B, L, MB_HEADS, MB_HDIM)
    mv = mv.reshape(B, L, MB_HEADS, MB_HDIM)
    o_mb = moba_fn(mq, mk, mv).reshape(B, L, MB_WIDTH)
    h = h + jnp.concatenate([o_hg, o_mb.astype(o_hg.dtype)], axis=-1) @ w_out
    u = rmsnorm(h, g_n2) @ w_up
    u_pad = jnp.concatenate([conv_prev.astype(u.dtype), u], axis=1)
    c = b_conv
    for j in range(CONV_W):
        c = c + w_conv[j] * u_pad[:, j:j + L]
    conv_new = u_pad[:, -(CONV_W - 1):]
    a, g = jnp.split(c, 2, axis=-1)
    h = h + (jax.nn.silu(g) * a) @ w_down
    gate = jax.nn.sigmoid(rmsnorm(h, g_pn) @ w_pg)
    h = h + gate * (pe @ w_ple)
    return h, s_new, conv_new, mk, mv


def setup_inputs(seed: int = 0) -> dict:
    key = jax.random.key(seed)
    ks = jax.random.split(key, 26)
    f32 = jnp.float32
    n_pages = PAST_LEN // PAGE_SIZE
    n_pool = (DEC_BATCH * n_pages * 5) // 4

    def nrm(k, shape, s):
        return jax.random.normal(k, shape, f32) * s

    def gain(k, shape):
        return 1.0 + 0.01 * jax.random.normal(k, shape, f32)

    page_table = jax.random.permutation(ks[6], n_pool)[:DEC_BATCH * n_pages].reshape(
        DEC_BATCH, n_pages).astype(jnp.int32)
    return {
        "x_prompt": nrm(ks[0], (BATCH, SEQ, D_MODEL), 1.0),
        "x_sample": nrm(ks[1], (DEC_BATCH, DEC_SEQ, D_MODEL), 1.0),
        "p_prompt": nrm(ks[2], (DEPTH, BATCH, SEQ, PLE_DIM), 1.0),
        "p_sample": nrm(ks[3], (DEPTH, DEC_BATCH, DEC_SEQ, PLE_DIM), 1.0),
        "cache_k": nrm(ks[4], (DEPTH, n_pool, PAGE_SIZE, MB_HEADS, MB_HDIM), 1.0),
        "cache_v": nrm(ks[5], (DEPTH, n_pool, PAGE_SIZE, MB_HEADS, MB_HDIM), 1.0),
        "page_table": page_table,
        "state_hgrn": nrm(ks[7], (DEPTH, DEC_BATCH, HG_HEADS, HG_KDIM, HG_VDIM), 0.3),
        "state_ffn_conv": nrm(ks[8], (DEPTH, DEC_BATCH, CONV_W - 1, 2 * D_FF), 1.0),
        "lb_logits": nrm(ks[9], (DEPTH + 1, HG_WIDTH), 0.1),
        "w_in": nrm(ks[10], (DEPTH, D_MODEL, IN_COLS), D_MODEL ** -0.5),
        "b_f": nrm(ks[11], (DEPTH, HG_WIDTH), 0.1),
        "g_onorm": gain(ks[12], (DEPTH, HG_VDIM)),
        "w_out": nrm(ks[13], (DEPTH, MIX_WIDTH, D_MODEL), MIX_WIDTH ** -0.5),
        "g_n1": gain(ks[14], (DEPTH, D_MODEL)),
        "g_n2": gain(ks[15], (DEPTH, D_MODEL)),
        "w_up": nrm(ks[16], (DEPTH, D_MODEL, 2 * D_FF), D_MODEL ** -0.5),
        "w_conv": nrm(ks[17], (DEPTH, CONV_W, 2 * D_FF), CONV_W ** -0.5),
        "b_conv": nrm(ks[18], (DEPTH, 2 * D_FF), 0.01),
        "w_down": nrm(ks[19], (DEPTH, D_FF, D_MODEL), D_FF ** -0.5),
        "w_ple": nrm(ks[20], (DEPTH, PLE_DIM, D_MODEL), PLE_DIM ** -0.5),
        "w_pg": nrm(ks[21], (DEPTH, D_MODEL, D_MODEL), D_MODEL ** -0.5),
        "g_pn": gain(ks[22], (DEPTH, D_MODEL)),
        "g_final": gain(ks[23], (D_MODEL,)),
    }


def reference(x_prompt, x_sample, p_prompt, p_sample, cache_k, cache_v, page_table, state_hgrn,
              state_ffn_conv, lb_logits, w_in, b_f, g_onorm, w_out, g_n1, g_n2, w_up, w_conv, b_conv,
              w_down, w_ple, w_pg, g_pn, g_final):
    slopes = alibi_slopes()
    lbs = jnp.cumsum(jax.nn.softmax(lb_logits.astype(jnp.float32), axis=0), axis=0)
    hp, hs = x_prompt, x_sample
    kp_l, vp_l, sp_l, cp_l, ks_l, vs_l, ss_l, cs_l = [], [], [], [], [], [], [], []
    for i in range(DEPTH):
        w = (lbs[i], w_in[i], b_f[i], g_onorm[i], w_out[i], g_n1[i], g_n2[i], w_up[i], w_conv[i],
             b_conv[i], w_down[i], w_ple[i], w_pg[i], g_pn[i])
        bp = hp.shape[0]
        s0 = jnp.zeros((bp, HG_HEADS, HG_KDIM, HG_VDIM), hp.dtype)
        c0 = jnp.zeros((bp, CONV_W - 1, 2 * D_FF), hp.dtype)
        hp, sp, cp, kp, vp = block_forward(hp, p_prompt[i], s0, c0,
                                           lambda q, k, v: moba_prompt(q, k, v, slopes), *w)
        hs, ss, cs, ksn, vsn = block_forward(
            hs, p_sample[i], state_hgrn[i], state_ffn_conv[i],
            lambda q, k, v: moba_sample(q, k, v, cache_k[i], cache_v[i], page_table, slopes), *w)
        kp_l.append(kp); vp_l.append(vp); sp_l.append(sp); cp_l.append(cp)
        ks_l.append(ksn); vs_l.append(vsn); ss_l.append(ss); cs_l.append(cs)
    y_prompt = rmsnorm(hp, g_final)
    y_sample = rmsnorm(hs, g_final)
    return (y_prompt, y_sample, jnp.stack(kp_l), jnp.stack(vp_l), jnp.stack(sp_l), jnp.stack(cp_l),
            jnp.stack(ks_l), jnp.stack(vs_l), jnp.stack(ss_l), jnp.stack(cs_l))
```

```python
import functools

import jax
import jax.numpy as jnp
from jax import lax
from jax.experimental import pallas as pl
from jax.experimental.pallas import tpu as pltpu

F32 = jnp.float32
BF16 = jnp.bfloat16

HEAD_DIM = 128
N_HEADS = 16
MOBA_BLOCK = 256
MOBA_TOPK = 3
GLA_CHUNK = 64
CONV_W = 3
PAGE_ROWS = 128
RMS_EPS = 1e-6
NEG_BIG = -1e30

V7X_VMEM_LIMIT_BYTES = 56 * 1024 * 1024
SUBLANES = 8
LANES = 128


def _params(*sem):
    return pltpu.CompilerParams(dimension_semantics=sem, vmem_limit_bytes=V7X_VMEM_LIMIT_BYTES)


def _dot(a, b):
    return jnp.dot(a, b, preferred_element_type=F32)


def _dot_nt(a, b):
    return lax.dot_general(a, b, (((1,), (1,)), ((), ())), preferred_element_type=F32)


def _dot_tn(a, b):
    return lax.dot_general(a, b, (((0,), (0,)), ((), ())), preferred_element_type=F32)


def _sigmoid_pair(x):
    e = jnp.exp(-jnp.abs(x))
    r = 1.0 / (1.0 + e)
    er = e * r
    pos = x >= 0
    return jnp.where(pos, r, er), jnp.where(pos, er, r)


def _silu(x):
    return x * _sigmoid_pair(x)[0]


def _rms_body(x_ref, g_ref, o_ref):
    x = x_ref[...]
    ms = jnp.mean(x * x, axis=-1, keepdims=True)
    o_ref[...] = (x * lax.rsqrt(ms + RMS_EPS) * g_ref[...]).astype(o_ref.dtype)


def rmsnorm_rows(x, g, out_dtype):
    m, d = x.shape
    tm = min(m, 256)
    return pl.pallas_call(
        _rms_body,
        out_shape=jax.ShapeDtypeStruct((m, d), out_dtype),
        grid=(m // tm,),
        in_specs=[pl.BlockSpec((tm, d), lambda i: (i, 0)), pl.BlockSpec((1, d), lambda i: (0, 0))],
        out_specs=pl.BlockSpec((tm, d), lambda i: (i, 0)),
        compiler_params=_params("parallel"),
        name="rmsnorm",
    )(x, g.reshape(1, d))


def _mm_plain_body(a_ref, w_ref, o_ref):
    o_ref[...] = _dot(a_ref[...], w_ref[...])


def _mm_res_body(a_ref, w_ref, r_ref, o_ref):
    o_ref[...] = r_ref[...] + _dot(a_ref[...], w_ref[...])


def _mm_gate_body(a_ref, w_ref, h_ref, pe_ref, wple_ref, o_ref):
    gate = _sigmoid_pair(_dot(a_ref[...], w_ref[...]))[0]
    o_ref[...] = h_ref[...] + gate * _dot(pe_ref[...], wple_ref[...])


def _mm_call(body, a, w, extra, extra_specs, tm, tn, single_buffer_a, name):
    m, k = a.shape
    n = w.shape[1]
    tm = min(tm, m)
    a_kw = dict(pipeline_mode=pl.Buffered(1)) if single_buffer_a else {}
    return pl.pallas_call(
        body,
        out_shape=jax.ShapeDtypeStruct((m, n), F32),
        grid=(m // tm, n // tn),
        in_specs=[pl.BlockSpec((tm, k), lambda i, j: (i, 0), **a_kw),
                  pl.BlockSpec((k, tn), lambda i, j: (0, j))] + extra_specs(tm, tn),
        out_specs=pl.BlockSpec((tm, tn), lambda i, j: (i, j)),
        compiler_params=_params("parallel", "arbitrary"),
        name=name,
    )(a, w, *extra)


def matmul(a, w, *, tm=1024, tn=1024, name="matmul"):
    return _mm_call(_mm_plain_body, a, w, (), lambda tm, tn: [], tm, tn, False, name)


def matmul_residual(a, w, res, *, tm=1024, tn=512, single_buffer_a=False, name="matmul_residual"):
    specs = lambda tm, tn: [pl.BlockSpec((tm, tn), lambda i, j: (i, j))]
    return _mm_call(_mm_res_body, a, w, (res,), specs, tm, tn, single_buffer_a, name)


def matmul_gate(a, w, h, pe, wple, *, tm=1024, tn=512, name="matmul_gate"):
    kp = pe.shape[1]
    specs = lambda tm, tn: [pl.BlockSpec((tm, tn), lambda i, j: (i, j)),
                            pl.BlockSpec((tm, kp), lambda i, j: (i, 0)),
                            pl.BlockSpec((kp, tn), lambda i, j: (0, j))]
    return _mm_call(_mm_gate_body, a, w, (h, pe, wple), specs, tm, tn, False, name)


def _split2_bf16(x):
    hi = x.astype(BF16)
    return hi, (x - hi.astype(F32)).astype(BF16)


def _mm_x3_body(a_ref, w_ref, o_ref):
    ah, al = _split2_bf16(a_ref[...])
    wh, wl = _split2_bf16(w_ref[...])
    o_ref[...] = (_dot(al, wh) + _dot(ah, wl)) + _dot(ah, wh)


def matmul_f32_cols(a, w, col0, n, *, tn=512, name="matmul_x3"):
    m, k = a.shape
    assert col0 % tn == 0 and n % tn == 0
    return pl.pallas_call(
        _mm_x3_body,
        out_shape=jax.ShapeDtypeStruct((m, n), F32),
        grid=(n // tn,),
        in_specs=[pl.BlockSpec((m, k), lambda j: (0, 0)),
                  pl.BlockSpec((k, tn), lambda j: (0, col0 // tn + j))],
        out_specs=pl.BlockSpec((m, tn), lambda j: (0, j)),
        compiler_params=_params("parallel"),
        name=name,
    )(a, w)


def _conv_taps(wc_ref, b_ref, um2, um1, u):
    wc = wc_ref[...]
    return b_ref[...] + wc[0:1] * um2 + wc[1:2] * um1 + wc[2:3] * u


def _upconv_prompt_body(x_ref, wa_ref, wg_ref, wca_ref, wcg_ref, ba_ref, bg_ref,
                        act_ref, cna_ref, cng_ref, sa_ref, sg_ref, ha_ref, hg_ref,
                        *, tiles_per_seq):
    i = pl.program_id(0)
    j = pl.program_id(1)
    tm = x_ref.shape[0]
    x = x_ref[...]
    seq_start = (i % tiles_per_seq) == 0

    def half(w_ref, wc_ref, b_ref, s_ref, h_ref, cn_ref):
        u = _dot(x, w_ref[...])
        @pl.when(seq_start)
        def _():
            s_ref[0:SUBLANES, :] = jnp.zeros((SUBLANES, u.shape[1]), F32)

        @pl.when(jnp.logical_not(seq_start))
        def _():
            s_ref[0:SUBLANES, :] = h_ref[j]

        s_ref[SUBLANES:, :] = u
        h_ref[j] = s_ref[tm:tm + SUBLANES, :]
        cn_ref[0] = s_ref[tm + SUBLANES - (CONV_W - 1):tm + SUBLANES, :]
        return _conv_taps(wc_ref, b_ref, s_ref[SUBLANES - 2:SUBLANES - 2 + tm, :],
                          s_ref[SUBLANES - 1:SUBLANES - 1 + tm, :], u)

    ca = half(wa_ref, wca_ref, ba_ref, sa_ref, ha_ref, cna_ref)
    cg = half(wg_ref, wcg_ref, bg_ref, sg_ref, hg_ref, cng_ref)
    act_ref[...] = (_silu(cg) * ca).astype(act_ref.dtype)


def upconv_prompt(xn, w_up, w_conv, b_conv, seq_len, *, tm=1024, tn=256):
    m, d = xn.shape
    dff = w_up.shape[1] // 2
    tm = min(tm, seq_len)
    nj = dff // tn
    n_seq = m // seq_len
    tiles_per_seq = seq_len // tm
    b2 = b_conv.reshape(1, 2 * dff)
    body = functools.partial(_upconv_prompt_body, tiles_per_seq=tiles_per_seq)
    act, cna, cng = pl.pallas_call(
        body,
        out_shape=(jax.ShapeDtypeStruct((m, dff), BF16),
                   jax.ShapeDtypeStruct((m // tm, CONV_W - 1, dff), F32),
                   jax.ShapeDtypeStruct((m // tm, CONV_W - 1, dff), F32)),
        grid=(m // tm, nj),
        in_specs=[pl.BlockSpec((tm, d), lambda i, j: (i, 0)),
                  pl.BlockSpec((d, tn), lambda i, j: (0, j)),
                  pl.BlockSpec((d, tn), lambda i, j: (0, j + nj)),
                  pl.BlockSpec((CONV_W, tn), lambda i, j: (0, j)),
                  pl.BlockSpec((CONV_W, tn), lambda i, j: (0, j + nj)),
                  pl.BlockSpec((1, tn), lambda i, j: (0, j)),
                  pl.BlockSpec((1, tn), lambda i, j: (0, j + nj))],
        out_specs=(pl.BlockSpec((tm, tn), lambda i, j: (i, j)),
                   pl.BlockSpec((1, CONV_W - 1, tn), lambda i, j: (i, 0, j)),
                   pl.BlockSpec((1, CONV_W - 1, tn), lambda i, j: (i, 0, j))),
        scratch_shapes=[pltpu.VMEM((tm + SUBLANES, tn), F32), pltpu.VMEM((tm + SUBLANES, tn), F32),
                        pltpu.VMEM((nj, SUBLANES, tn), F32), pltpu.VMEM((nj, SUBLANES, tn), F32)],
        compiler_params=_params("arbitrary", "arbitrary"),
        name="upconv_prompt",
    )(xn, w_up, w_up, w_conv, w_conv, b2, b2)
    last = slice(tiles_per_seq - 1, None, tiles_per_seq)
    return act, jnp.concatenate([cna[last], cng[last]], axis=-1)


def _upconv_sample_body(x_ref, wa_ref, wg_ref, wca_ref, wcg_ref, ba_ref, bg_ref, pa_ref, pg_ref,
                        act_ref, cna_ref, cng_ref):
    x = x_ref[...]

    def half(w_ref, wc_ref, b_ref, p_ref, cn_ref):
        u = _dot(x, w_ref[...])
        cn_ref[:, 0, :] = p_ref[:, 1, :]
        cn_ref[:, 1, :] = u
        return _conv_taps(wc_ref, b_ref, p_ref[:, 0, :], p_ref[:, 1, :], u)

    ca = half(wa_ref, wca_ref, ba_ref, pa_ref, cna_ref)
    cg = half(wg_ref, wcg_ref, bg_ref, pg_ref, cng_ref)
    act_ref[...] = (_silu(cg) * ca).astype(act_ref.dtype)


def upconv_sample(xn, w_up, w_conv, b_conv, conv_prev, *, tn=256):
    m, d = xn.shape
    dff = w_up.shape[1] // 2
    nj = dff // tn
    b2 = b_conv.reshape(1, 2 * dff)
    act, cna, cng = pl.pallas_call(
        _upconv_sample_body,
        out_shape=(jax.ShapeDtypeStruct((m, dff), BF16),
                   jax.ShapeDtypeStruct((m, CONV_W - 1, dff), F32),
                   jax.ShapeDtypeStruct((m, CONV_W - 1, dff), F32)),
        grid=(nj,),
        in_specs=[pl.BlockSpec((m, d), lambda j: (0, 0)),
                  pl.BlockSpec((d, tn), lambda j: (0, j)),
                  pl.BlockSpec((d, tn), lambda j: (0, j + nj)),
                  pl.BlockSpec((CONV_W, tn), lambda j: (0, j)),
                  pl.BlockSpec((CONV_W, tn), lambda j: (0, j + nj)),
                  pl.BlockSpec((1, tn), lambda j: (0, j)),
                  pl.BlockSpec((1, tn), lambda j: (0, j + nj)),
                  pl.BlockSpec((m, CONV_W - 1, tn), lambda j: (0, 0, j)),
                  pl.BlockSpec((m, CONV_W - 1, tn), lambda j: (0, 0, j + nj))],
        out_specs=(pl.BlockSpec((m, tn), lambda j: (0, j)),
                   pl.BlockSpec((m, CONV_W - 1, tn), lambda j: (0, 0, j)),
                   pl.BlockSpec((m, CONV_W - 1, tn), lambda j: (0, 0, j))),
        compiler_params=_params("parallel"),
        name="upconv_sample",
    )(xn, w_up, w_up, w_conv, w_conv, b2, b2, conv_prev, conv_prev)
    return act, jnp.concatenate([cna, cng], axis=-1)


def _split3_bf16(x):
    hi = x.astype(BF16)
    r1 = x - hi.astype(F32)
    mid = r1.astype(BF16)
    lo = (r1 - mid.astype(F32)).astype(BF16)
    return hi, mid, lo


def _gla_prompt_body(zq_ref, zf_ref, zi_ref, zg_ref, bf_ref, lb_ref, gon_ref, o_ref, s_ref, st_ref):
    c = pl.program_id(1)
    n_chunks = pl.num_programs(1)
    C = zq_ref.shape[0]

    @pl.when(c == 0)
    def _():
        st_ref[...] = jnp.zeros(st_ref.shape, F32)

    lb = lb_ref[...]
    sig, nsig = _sigmoid_pair(zf_ref[...] + bf_ref[...])
    logf = jnp.log(lb + (1.0 - lb) * sig)
    key = (1.0 - lb) * nsig

    row = lax.broadcasted_iota(jnp.int32, (C, C), 0)
    col = lax.broadcasted_iota(jnp.int32, (C, C), 1)
    causal = row >= col
    tri = causal.astype(BF16)
    hi, mid, lo = _split3_bf16(logf)
    cum = (_dot(tri, lo) + _dot(tri, mid)) + _dot(tri, hi)

    last = cum[C - 1:C, :]
    q_dec = zq_ref[...] * jnp.exp(cum)
    k_inv = key * jnp.exp(-cum)
    k_last = key * jnp.exp(last - cum)
    decay = jnp.exp(last)
    gon = gon_ref[...]

    for h in range(N_HEADS):
        sl = slice(h * HEAD_DIM, (h + 1) * HEAD_DIM)
        qh = q_dec[:, sl].astype(BF16)
        vh = zi_ref[:, sl].astype(BF16)
        att = jnp.where(causal, _dot_nt(qh, k_inv[:, sl].astype(BF16)), 0.0)
        st = st_ref[h]
        o = _dot_nt(qh, st.astype(BF16)) + _dot(att.astype(BF16), vh)
        st_ref[h] = decay[:, sl] * st + _dot_tn(vh, k_last[:, sl].astype(BF16))
        ms = jnp.mean(o * o, axis=-1, keepdims=True)
        on = o * lax.rsqrt(ms + RMS_EPS) * gon
        o_ref[:, sl] = (on * _silu(zg_ref[:, sl])).astype(o_ref.dtype)

    @pl.when(c == n_chunks - 1)
    def _():
        for h in range(N_HEADS):
            s_ref[0, h] = st_ref[h].T


def hgrn2_prompt(z, seq_len, b_f, lb, g_onorm):
    m = z.shape[0]
    w = N_HEADS * HEAD_DIM
    n_seq = m // seq_len
    C = GLA_CHUNK if seq_len % GLA_CHUNK == 0 else seq_len
    nc = seq_len // C
    zspec = lambda g: pl.BlockSpec((C, w), lambda b, c: (b * nc + c, g))
    vec = pl.BlockSpec((1, w), lambda b, c: (0, 0))
    return pl.pallas_call(
        _gla_prompt_body,
        out_shape=(jax.ShapeDtypeStruct((m, w), BF16),
                   jax.ShapeDtypeStruct((n_seq, N_HEADS, HEAD_DIM, HEAD_DIM), F32)),
        grid=(n_seq, nc),
        in_specs=[zspec(0), zspec(1), zspec(2), zspec(3), vec, vec,
                  pl.BlockSpec((1, HEAD_DIM), lambda b, c: (0, 0))],
        out_specs=(pl.BlockSpec((C, w), lambda b, c: (b * nc + c, 0)),
                   pl.BlockSpec((1, N_HEADS, HEAD_DIM, HEAD_DIM), lambda b, c: (b, 0, 0, 0))),
        scratch_shapes=[pltpu.VMEM((N_HEADS, HEAD_DIM, HEAD_DIM), F32)],
        compiler_params=_params("parallel", "arbitrary"),
        name="hgrn2_prompt",
    )(z, z, z, z, b_f.reshape(1, w), lb.reshape(1, w), g_onorm.reshape(1, HEAD_DIM))


def _hgrn_sample_body(zqT_ref, zfT_ref, zi_ref, zg_ref, bfT_ref, lbT_ref, gon_ref, s_ref,
                      o_ref, so_ref):
    nb = zi_ref.shape[0]
    lb = lbT_ref[...]
    sig, nsig = _sigmoid_pair(zfT_ref[...] + bfT_ref[...])
    f = lb + (1.0 - lb) * sig
    key = (1.0 - lb) * nsig
    q = zqT_ref[...]
    v = zi_ref[...]
    rows = []
    for b in range(nb):
        s_new = f[:, b:b + 1] * s_ref[b, 0] + key[:, b:b + 1] * v[b:b + 1, :]
        so_ref[b, 0] = s_new
        rows.append(jnp.sum(q[:, b:b + 1] * s_new, axis=0, keepdims=True))
    o = jnp.concatenate(rows, axis=0)
    ms = jnp.mean(o * o, axis=-1, keepdims=True)
    on = o * lax.rsqrt(ms + RMS_EPS) * gon_ref[...]
    o_ref[...] = (on * _silu(zg_ref[...])).astype(o_ref.dtype)


def hgrn2_sample(z, b_f, lb, g_onorm, s0):
    nb = z.shape[0]
    w = N_HEADS * HEAD_DIM
    zt = z[:, :2 * w].T
    colspec = lambda g: pl.BlockSpec((HEAD_DIM, nb), lambda h: (g * N_HEADS + h, 0))
    rowspec = lambda g: pl.BlockSpec((nb, HEAD_DIM), lambda h: (0, g * N_HEADS + h))
    vecT = pl.BlockSpec((HEAD_DIM, 1), lambda h: (h, 0))
    sspec = pl.BlockSpec((nb, 1, HEAD_DIM, HEAD_DIM), lambda h: (0, h, 0, 0))
    return pl.pallas_call(
        _hgrn_sample_body,
        out_shape=(jax.ShapeDtypeStruct((nb, w), BF16), jax.ShapeDtypeStruct(s0.shape, F32)),
        grid=(N_HEADS,),
        in_specs=[colspec(0), colspec(1), rowspec(2), rowspec(3), vecT, vecT,
                  pl.BlockSpec((1, HEAD_DIM), lambda h: (0, 0)), sspec],
        out_specs=(pl.BlockSpec((nb, HEAD_DIM), lambda h: (0, h)), sspec),
        compiler_params=_params("parallel"),
        name="hgrn2_sample",
    )(zt, zt, z, z, b_f.reshape(w, 1), lb.reshape(w, 1), g_onorm.reshape(1, HEAD_DIM), s0)


def _moba_prompt_body(slopes_ref, q_ref, k_ref, v_ref, o_ref, kbf_ref, vbf_ref, kmh_ref, kml_ref,
                      *, n_blocks):
    h = pl.program_id(1)
    qi = pl.program_id(2)
    T = MOBA_BLOCK

    @pl.when(qi == 0)
    def _():
        k = k_ref[...]
        kbf_ref[...] = k.astype(BF16)
        vbf_ref[...] = v_ref[...].astype(BF16)
        sums = [jnp.sum(k[b * T:(b + 1) * T], axis=0, keepdims=True) for b in range(n_blocks)]
        sums.append(jnp.zeros((LANES - n_blocks, HEAD_DIM), F32))
        km = jnp.concatenate(sums, axis=0) * (1.0 / T)
        hi = km.astype(BF16)
        kmh_ref[...] = hi
        kml_ref[...] = (km - hi.astype(F32)).astype(BF16)

    q = q_ref[...]
    qb = q.astype(BF16)
    ql = (q - qb.astype(F32)).astype(BF16)
    sc = (_dot_nt(ql, kmh_ref[...]) + _dot_nt(qb, kml_ref[...])) + _dot_nt(qb, kmh_ref[...])
    lane = lax.broadcasted_iota(jnp.int32, sc.shape, 1)
    sc = jnp.where(lane < qi, sc, -jnp.inf)
    sel = jnp.zeros(sc.shape, F32)
    for _ in range(MOBA_TOPK):
        mx = jnp.max(sc, axis=-1, keepdims=True)
        hit = jnp.logical_and(sc == mx, mx > -jnp.inf)
        first = jnp.min(jnp.where(hit, lane, LANES), axis=-1, keepdims=True)
        pick = lane == first
        sel = jnp.where(pick, 1.0, sel)
        sc = jnp.where(pick, -jnp.inf, sc)

    slope = slopes_ref[h]
    scale = HEAD_DIM ** -0.5
    r = lax.broadcasted_iota(jnp.int32, (T, T), 0)
    cidx = lax.broadcasted_iota(jnp.int32, (T, T), 1)
    rel = (r - cidx).astype(F32)

    own = pl.multiple_of(qi * T, T)
    s = _dot_nt(qb, kbf_ref[pl.ds(own, T), :]) * scale - slope * rel
    s = jnp.where(r >= cidx, s, NEG_BIG)
    m0 = jnp.max(s, axis=-1, keepdims=True)
    p = jnp.exp(s - m0)
    l0 = jnp.sum(p, axis=-1, keepdims=True)
    acc0 = _dot(p.astype(BF16), vbf_ref[pl.ds(own, T), :])

    def past_block(i, carry):
        m, l, acc = carry
        start = pl.multiple_of(i * T, T)
        dist = rel + ((qi - i) * T).astype(F32)
        s = _dot_nt(qb, kbf_ref[pl.ds(start, T), :]) * scale - slope * dist
        chosen = jnp.sum(jnp.where(lane == i, sel, 0.0), axis=-1, keepdims=True) > 0.0
        s = jnp.where(chosen, s, NEG_BIG)
        m_new = jnp.maximum(m, jnp.max(s, axis=-1, keepdims=True))
        alpha = jnp.exp(m - m_new)
        p = jnp.exp(s - m_new)
        l = alpha * l + jnp.sum(p, axis=-1, keepdims=True)
        acc = alpha * acc + _dot(p.astype(BF16), vbf_ref[pl.ds(start, T), :])
        return m_new, l, acc

    m, l, acc = lax.fori_loop(0, qi, past_block, (m0, l0, acc0))
    o_ref[...] = (acc / l).astype(o_ref.dtype)


def moba_prompt(z, seq_len, slopes, col0):
    m = z.shape[0]
    n_seq = m // seq_len
    T = MOBA_BLOCK
    assert seq_len % T == 0
    nq = seq_len // T
    body = functools.partial(_moba_prompt_body, n_blocks=nq)
    return pl.pallas_call(
        body,
        out_shape=jax.ShapeDtypeStruct((m, N_HEADS * HEAD_DIM), BF16),
        grid=(n_seq, N_HEADS, nq),
        in_specs=[pl.BlockSpec(memory_space=pltpu.SMEM),
                  pl.BlockSpec((T, HEAD_DIM), lambda b, h, i: (b * nq + i, col0 + h)),
                  pl.BlockSpec((seq_len, HEAD_DIM), lambda b, h, i: (b, col0 + N_HEADS + h)),
                  pl.BlockSpec((seq_len, HEAD_DIM), lambda b, h, i: (b, col0 + 2 * N_HEADS + h))],
        out_specs=pl.BlockSpec((T, HEAD_DIM), lambda b, h, i: (b * nq + i, h)),
        scratch_shapes=[pltpu.VMEM((seq_len, HEAD_DIM), BF16), pltpu.VMEM((seq_len, HEAD_DIM), BF16),
                        pltpu.VMEM((LANES, HEAD_DIM), BF16), pltpu.VMEM((LANES, HEAD_DIM), BF16)],
        compiler_params=_params("parallel", "parallel", "arbitrary"),
        name="moba_prompt",
    )(slopes, z, z, z)


def _page_sum_body(pt_ref, p0_ref, p1_ref, o_ref):
    o_ref[0, 0] = (jnp.sum(p0_ref[0], axis=0, keepdims=True)
                   + jnp.sum(p1_ref[0], axis=0, keepdims=True))


def moba_block_key_sums(cache_k, page_table):
    n_pool = cache_k.shape[0]
    w = N_HEADS * HEAD_DIM
    nb, n_pages = page_table.shape
    assert MOBA_BLOCK == 2 * PAGE_ROWS
    n_blk = n_pages // 2
    ck = cache_k.reshape(n_pool, PAGE_ROWS, w)
    grid_spec = pltpu.PrefetchScalarGridSpec(
        num_scalar_prefetch=1,
        grid=(nb, n_blk),
        in_specs=[pl.BlockSpec((1, PAGE_ROWS, w), lambda b, j, pt: (pt[b, 2 * j], 0, 0)),
                  pl.BlockSpec((1, PAGE_ROWS, w), lambda b, j, pt: (pt[b, 2 * j + 1], 0, 0))],
        out_specs=pl.BlockSpec((1, 1, 1, w), lambda b, j, pt: (b, j, 0, 0)),
    )
    return pl.pallas_call(
        _page_sum_body,
        out_shape=jax.ShapeDtypeStruct((nb, n_blk, 1, w), F32),
        grid_spec=grid_spec,
        compiler_params=_params("parallel", "parallel"),
        name="moba_page_sums",
    )(page_table, ck, ck)


def _sample_topk_body(ks_ref, q_ref, idx_ref, ok_ref):
    n_blk = ks_ref.shape[1]
    km = ks_ref[0, :, 0, :] * (1.0 / MOBA_BLOCK)
    prod = km * q_ref[0]
    lane = lax.broadcasted_iota(jnp.int32, (n_blk, LANES), 1)
    blk = lax.broadcasted_iota(jnp.int32, (n_blk, LANES), 0)
    sc = jnp.full((n_blk, LANES), -jnp.inf, F32)
    for h in range(N_HEADS):
        col = jnp.sum(prod[:, h * HEAD_DIM:(h + 1) * HEAD_DIM], axis=-1, keepdims=True)
        sc = jnp.where(lane == h, col, sc)
    idxs, oks = [], []
    for _ in range(MOBA_TOPK):
        mx = jnp.max(sc, axis=0, keepdims=True)
        first = jnp.min(jnp.where(sc == mx, blk, n_blk), axis=0, keepdims=True)
        idxs.append(first)
        oks.append(jnp.logical_and(mx > -jnp.inf, mx < jnp.inf).astype(jnp.int32))
        sc = jnp.where(blk == first, -jnp.inf, sc)
    idx_ref[0] = jnp.concatenate(idxs, axis=0)
    ok_ref[0] = jnp.concatenate(oks, axis=0)


def moba_sample_topk(key_sums, q3):
    nb, n_blk, _, w = key_sums.shape
    assert n_blk >= MOBA_TOPK
    out = jax.ShapeDtypeStruct((nb, MOBA_TOPK, LANES), jnp.int32)
    ospec = pl.BlockSpec((1, MOBA_TOPK, LANES), lambda b: (b, 0, 0))
    return pl.pallas_call(
        _sample_topk_body,
        out_shape=(out, out),
        grid=(nb,),
        in_specs=[pl.BlockSpec((1, n_blk, 1, w), lambda b: (b, 0, 0, 0)),
                  pl.BlockSpec((1, 1, w), lambda b: (b, 0, 0))],
        out_specs=(ospec, ospec),
        compiler_params=_params("parallel"),
        name="moba_sample_topk",
    )(key_sums, q3)


def _sample_attn_body(pt_ref, idx_ref, ok_ref, q_ref, kn_ref, vn_ref, ck_ref, cv_ref, o_ref,
                      kbuf, vbuf, sem, *, past_len):
    b = pl.program_id(0)
    T = MOBA_BLOCK
    pages_per_blk = T // PAGE_ROWS
    n_sel = MOBA_TOPK
    scale = HEAD_DIM ** -0.5

    def copies(h):
        out = []
        for j in range(n_sel):
            blk = idx_ref[b, j * N_HEADS + h]
            for r in range(pages_per_blk):
                page = pt_ref[b, blk * pages_per_blk + r]
                dst = pl.ds((j * pages_per_blk + r) * PAGE_ROWS, PAGE_ROWS)
                out.append(pltpu.make_async_copy(ck_ref.at[page, :, h, :], kbuf.at[h, dst, :], sem.at[0, h]))
                out.append(pltpu.make_async_copy(cv_ref.at[page, :, h, :], vbuf.at[h, dst, :], sem.at[1, h]))
        return out

    for h in range(N_HEADS):
        for cp in copies(h):
            cp.start()

    q = q_ref[0]
    kn = kn_ref[0]
    vn = vn_ref[0]
    colid = lax.broadcasted_iota(jnp.int32, (SUBLANES, n_sel * T), 1)
    for h in range(N_HEADS):
        for cp in copies(h):
            cp.wait()
        sl = slice(h * HEAD_DIM, (h + 1) * HEAD_DIM)
        slope = 2.0 ** (-8.0 * (h + 1) / N_HEADS)
        qh = q[:, sl]
        q8 = jnp.broadcast_to(qh, (SUBLANES, HEAD_DIM)).astype(BF16)
        s = _dot_nt(q8, kbuf[h].astype(BF16)) * scale
        dist = jnp.zeros(s.shape, F32)
        valid = jnp.zeros(s.shape, jnp.bool_)
        for j in range(n_sel):
            in_j = jnp.logical_and(colid >= j * T, colid < (j + 1) * T)
            key_pos = idx_ref[b, j * N_HEADS + h] * T - j * T + colid
            dist = jnp.where(in_j, (past_len - key_pos).astype(F32), dist)
            valid = jnp.logical_or(valid, jnp.logical_and(in_j, ok_ref[b, j * N_HEADS + h] > 0))
        s = jnp.where(valid, s - slope * dist, NEG_BIG)
        s_new = jnp.sum(qh * kn[:, sl], axis=-1, keepdims=True) * scale
        m = jnp.maximum(jnp.max(s, axis=-1, keepdims=True), s_new)
        p = jnp.exp(s - m)
        p_new = jnp.exp(s_new - m)
        l = jnp.sum(p, axis=-1, keepdims=True) + p_new
        acc = _dot(p.astype(BF16), vbuf[h].astype(BF16)) + p_new * vn[:, sl]
        o_ref[0, :, sl] = (acc[0:1] / l[0:1]).astype(o_ref.dtype)


def moba_sample_attend(q3, z3, cache_k, cache_v, page_table, idx, ok, kcol, past_len):
    nb = z3.shape[0]
    w = N_HEADS * HEAD_DIM
    rows = MOBA_TOPK * MOBA_BLOCK
    zspec = lambda g: pl.BlockSpec((1, 1, w), lambda b, *_: (b, 0, kcol + g))
    grid_spec = pltpu.PrefetchScalarGridSpec(
        num_scalar_prefetch=3,
        grid=(nb,),
        in_specs=[pl.BlockSpec((1, 1, w), lambda b, *_: (b, 0, 0)), zspec(0), zspec(1),
                  pl.BlockSpec(memory_space=pl.ANY), pl.BlockSpec(memory_space=pl.ANY)],
        out_specs=pl.BlockSpec((1, 1, w), lambda b, *_: (b, 0, 0)),
        scratch_shapes=[pltpu.VMEM((N_HEADS, rows, HEAD_DIM), F32),
                        pltpu.VMEM((N_HEADS, rows, HEAD_DIM), F32),
                        pltpu.SemaphoreType.DMA((2, N_HEADS))],
    )
    out = pl.pallas_call(
        functools.partial(_sample_attn_body, past_len=past_len),
        out_shape=jax.ShapeDtypeStruct((nb, 1, w), BF16),
        grid_spec=grid_spec,
        compiler_params=_params("arbitrary"),
        name="moba_sample_attend",
    )(page_table, idx, ok, q3, z3, z3, cache_k, cache_v)
    return out.reshape(nb, w)


def _alibi_slopes():
    return 2.0 ** (-8.0 * jnp.arange(1, N_HEADS + 1, dtype=F32) / N_HEADS)


def kernel(x_prompt, x_sample, p_prompt, p_sample, cache_k, cache_v, page_table, state_hgrn,
           state_ffn_conv, lb_logits, w_in, b_f, g_onorm, w_out, g_n1, g_n2, w_up, w_conv, b_conv,
           w_down, w_ple, w_pg, g_pn, g_final):
    depth = w_in.shape[0]
    assert depth == 1
    B, L, D = x_prompt.shape
    Bd, Ld, _ = x_sample.shape
    assert Ld == 1
    hgw = N_HEADS * HEAD_DIM
    n_pages = page_table.shape[1]
    past_len = n_pages * PAGE_ROWS
    moba_col128 = 4 * N_HEADS
    moba_colw = 4

    lbs = jnp.cumsum(jax.nn.softmax(lb_logits.astype(F32), axis=0), axis=0)
    slopes = _alibi_slopes()
    i = 0
    lb = lbs[i]
    win, wout, wup, wdown, wpg, wple = (a[i].astype(BF16) for a in (w_in, w_out, w_up, w_down, w_pg, w_ple))

    def tail(h0, attn_cat, pe, upconv):
        h1 = matmul_residual(attn_cat, wout, h0, name="out_proj")
        act, conv_new = upconv(rmsnorm_rows(h1, g_n2[i], BF16))
        h2 = matmul_residual(act, wdown, h1, tn=256, single_buffer_a=True, name="down_proj")
        h3 = matmul_gate(rmsnorm_rows(h2, g_pn[i], BF16), wpg, h2, pe.astype(BF16), wple)
        return rmsnorm_rows(h3, g_final, F32), conv_new

    xp = x_prompt.reshape(B * L, D)
    zp = matmul(rmsnorm_rows(xp, g_n1[i], BF16), win, name="in_proj")
    o_hg, s_prompt = hgrn2_prompt(zp, L, b_f[i], lb, g_onorm[i])
    o_mb = moba_prompt(zp, L, slopes, moba_col128)
    y_p, conv_p = tail(xp, jnp.concatenate([o_hg, o_mb], axis=-1), p_prompt[i].reshape(B * L, -1),
                       lambda xn: upconv_prompt(xn, wup, w_conv[i], b_conv[i], L))
    k_p = zp[:, 5 * hgw:6 * hgw].reshape(1, B, L, N_HEADS, HEAD_DIM)
    v_p = zp[:, 6 * hgw:7 * hgw].reshape(1, B, L, N_HEADS, HEAD_DIM)

    xs = x_sample.reshape(Bd, D)
    xs_n = rmsnorm_rows(xs, g_n1[i], F32)
    zs = matmul(xs_n.astype(BF16), win, name="in_proj_s")
    o_hg_s, s_sample = hgrn2_sample(zs, b_f[i], lb, g_onorm[i], state_hgrn[i])
    zs3 = zs.reshape(Bd, 1, -1)
    qs3 = matmul_f32_cols(xs_n, w_in[i], moba_colw * hgw, hgw, name="in_proj_s_q").reshape(Bd, 1, hgw)
    key_sums = moba_block_key_sums(cache_k[i], page_table)
    idx, ok = moba_sample_topk(key_sums, qs3)
    o_mb_s = moba_sample_attend(qs3, zs3, cache_k[i], cache_v[i], page_table,
                                idx[:, :, :N_HEADS].reshape(Bd, -1), ok[:, :, :N_HEADS].reshape(Bd, -1),
                                moba_colw + 1, past_len)
    y_s, conv_s = tail(xs, jnp.concatenate([o_hg_s, o_mb_s], axis=-1), p_sample[i].reshape(Bd, -1),
                       lambda xn: upconv_sample(xn, wup, w_conv[i], b_conv[i], state_ffn_conv[i]))
    k_s = zs[:, 5 * hgw:6 * hgw].reshape(1, Bd, 1, N_HEADS, HEAD_DIM)
    v_s = zs[:, 6 * hgw:7 * hgw].reshape(1, Bd, 1, N_HEADS, HEAD_DIM)

    return (y_p.reshape(B, L, D), y_s.reshape(Bd, 1, D), k_p, v_p, s_prompt[None], conv_p[None],
            k_s, v_s, s_sample[None], conv_s[None])
```

```python
import functools

import jax
import jax.numpy as jnp
from jax import lax
from jax.experimental import pallas as pl
from jax.experimental.pallas import tpu as pltpu

F32 = jnp.float32
BF16 = jnp.bfloat16

HEAD_DIM = 128
N_HEADS = 16
MOBA_BLOCK = 256
MOBA_TOPK = 3
GLA_CHUNK = 64
CONV_W = 3
PAGE_ROWS = 128
RMS_EPS = 1e-6
NEG_BIG = -1e30

V7X_VMEM_LIMIT_BYTES = 56 * 1024 * 1024
SUBLANES = 8
LANES = 128


def _params(*sem):
    return pltpu.CompilerParams(dimension_semantics=sem, vmem_limit_bytes=V7X_VMEM_LIMIT_BYTES)


def _dot(a, b):
    return jnp.dot(a, b, preferred_element_type=F32)


def _dot_nt(a, b):
    return lax.dot_general(a, b, (((1,), (1,)), ((), ())), preferred_element_type=F32)


def _dot_tn(a, b):
    return lax.dot_general(a, b, (((0,), (0,)), ((), ())), preferred_element_type=F32)


def _sigmoid_pair(x):
    e = jnp.exp(-jnp.abs(x))
    r = 1.0 / (1.0 + e)
    er = e * r
    pos = x >= 0
    return jnp.where(pos, r, er), jnp.where(pos, er, r)


def _silu(x):
    return x * _sigmoid_pair(x)[0]


def _rms_body(x_ref, g_ref, o_ref):
    x = x_ref[...]
    ms = jnp.mean(x * x, axis=-1, keepdims=True)
    o_ref[...] = (x * lax.rsqrt(ms + RMS_EPS) * g_ref[...]).astype(o_ref.dtype)


def rmsnorm_rows(x, g, out_dtype):
    m, d = x.shape
    tm = min(m, 256)
    return pl.pallas_call(
        _rms_body,
        out_shape=jax.ShapeDtypeStruct((m, d), out_dtype),
        grid=(m // tm,),
        in_specs=[pl.BlockSpec((tm, d), lambda i: (i, 0)), pl.BlockSpec((1, d), lambda i: (0, 0))],
        out_specs=pl.BlockSpec((tm, d), lambda i: (i, 0)),
        compiler_params=_params("parallel"),
        name="rmsnorm",
    )(x, g.reshape(1, d))


def _mm_plain_body(a_ref, w_ref, o_ref):
    o_ref[...] = _dot(a_ref[...], w_ref[...])


def _mm_res_body(a_ref, w_ref, r_ref, o_ref):
    o_ref[...] = r_ref[...] + _dot(a_ref[...], w_ref[...])


def _mm_gate_body(a_ref, w_ref, h_ref, pe_ref, wple_ref, o_ref):
    gate = _sigmoid_pair(_dot(a_ref[...], w_ref[...]))[0]
    o_ref[...] = h_ref[...] + gate * _dot(pe_ref[...], wple_ref[...])


def _mm_call(body, a, w, extra, extra_specs, tm, tn, single_buffer_a, name):
    m, k = a.shape
    n = w.shape[1]
    tm = min(tm, m)
    a_kw = dict(pipeline_mode=pl.Buffered(1)) if single_buffer_a else {}
    return pl.pallas_call(
        body,
        out_shape=jax.ShapeDtypeStruct((m, n), F32),
        grid=(m // tm, n // tn),
        in_specs=[pl.BlockSpec((tm, k), lambda i, j: (i, 0), **a_kw),
                  pl.BlockSpec((k, tn), lambda i, j: (0, j))] + extra_specs(tm, tn),
        out_specs=pl.BlockSpec((tm, tn), lambda i, j: (i, j)),
        compiler_params=_params("parallel", "arbitrary"),
        name=name,
    )(a, w, *extra)


def matmul(a, w, *, tm=1024, tn=1024, name="matmul"):
    return _mm_call(_mm_plain_body, a, w, (), lambda tm, tn: [], tm, tn, False, name)


def matmul_residual(a, w, res, *, tm=1024, tn=512, single_buffer_a=False, name="matmul_residual"):
    specs = lambda tm, tn: [pl.BlockSpec((tm, tn), lambda i, j: (i, j))]
    return _mm_call(_mm_res_body, a, w, (res,), specs, tm, tn, single_buffer_a, name)


def matmul_gate(a, w, h, pe, wple, *, tm=1024, tn=512, name="matmul_gate"):
    kp = pe.shape[1]
    specs = lambda tm, tn: [pl.BlockSpec((tm, tn), lambda i, j: (i, j)),
                            pl.BlockSpec((tm, kp), lambda i, j: (i, 0)),
                            pl.BlockSpec((kp, tn), lambda i, j: (0, j))]
    return _mm_call(_mm_gate_body, a, w, (h, pe, wple), specs, tm, tn, False, name)


def _split2_bf16(x):
    hi = x.astype(BF16)
    return hi, (x - hi.astype(F32)).astype(BF16)


def _mm_x3_body(a_ref, w_ref, o_ref):
    ah, al = _split2_bf16(a_ref[...])
    wh, wl = _split2_bf16(w_ref[...])
    o_ref[...] = (_dot(al, wh) + _dot(ah, wl)) + _dot(ah, wh)


def matmul_f32_cols(a, w, col0, n, *, tn=512, name="matmul_x3"):
    m, k = a.shape
    assert col0 % tn == 0 and n % tn == 0
    return pl.pallas_call(
        _mm_x3_body,
        out_shape=jax.ShapeDtypeStruct((m, n), F32),
        grid=(n // tn,),
        in_specs=[pl.BlockSpec((m, k), lambda j: (0, 0)),
                  pl.BlockSpec((k, tn), lambda j: (0, col0 // tn + j))],
        out_specs=pl.BlockSpec((m, tn), lambda j: (0, j)),
        compiler_params=_params("parallel"),
        name=name,
    )(a, w)


UP_TN = 256


def pair_up_columns(a, tn=UP_TN):
    lead, n = a.shape[:-1], a.shape[-1]
    nj = n // (2 * tn)
    return a.reshape(lead + (2, nj, tn)).swapaxes(-3, -2).reshape(lead + (n,))


def _conv_gate(wc, b, um2, um1, u, tn):
    c = b + wc[0:1] * um2 + wc[1:2] * um1 + wc[2:3] * u
    return _silu(c[:, tn:]) * c[:, :tn]


def _upconv_prompt_body(x_ref, w_ref, wc_ref, b_ref, act_ref, cna_ref, cng_ref, *u_refs):
    tm = x_ref.shape[0]
    tn = act_ref.shape[1]
    n_sub = len(u_refs)
    sub = tm // n_sub
    H = SUBLANES
    u_refs[0][0:H, :] = jnp.zeros((H, 2 * tn), F32)
    w = w_ref[...]
    wc = wc_ref[...]
    b = b_ref[...]

    def project(r):
        u = _dot(x_ref[r * sub:(r + 1) * sub, :], w)
        u_refs[r][H:, :] = u
        if r + 1 < n_sub:
            u_refs[r + 1][0:H, :] = u[sub - H:, :]
        else:
            cna_ref[0] = u[sub - (CONV_W - 1):, 0:tn]
            cng_ref[0] = u[sub - (CONV_W - 1):, tn:]

    def conv_gate(r):
        u_ref = u_refs[r]
        act = _conv_gate(wc, b, u_ref[H - 2:H - 2 + sub, :], u_ref[H - 1:H - 1 + sub, :],
                         u_ref[H:H + sub, :], tn)
        act_ref[r * sub:(r + 1) * sub, :] = act.astype(act_ref.dtype)

    project(0)
    for r in range(n_sub):
        if r + 1 < n_sub:
            project(r + 1)
        conv_gate(r)


def upconv_prompt(xn, w_up_p, w_conv_p, b_conv_p, seq_len, *, tn=UP_TN, sub=512):
    m, d = xn.shape
    dff = w_up_p.shape[1] // 2
    tm = seq_len
    sub = min(sub, tm)
    nj = dff // tn
    n_seq = m // seq_len
    cn = jax.ShapeDtypeStruct((n_seq, CONV_W - 1, dff), F32)
    cn_spec = pl.BlockSpec((1, CONV_W - 1, tn), lambda i, j: (i, 0, j))
    act, cna, cng = pl.pallas_call(
        _upconv_prompt_body,
        out_shape=(jax.ShapeDtypeStruct((m, dff), BF16), cn, cn),
        grid=(n_seq, nj),
        in_specs=[pl.BlockSpec((tm, d), lambda i, j: (i, 0), pipeline_mode=pl.Buffered(1)),
                  pl.BlockSpec((d, 2 * tn), lambda i, j: (0, j)),
                  pl.BlockSpec((CONV_W, 2 * tn), lambda i, j: (0, j)),
                  pl.BlockSpec((1, 2 * tn), lambda i, j: (0, j))],
        out_specs=(pl.BlockSpec((tm, tn), lambda i, j: (i, j)), cn_spec, cn_spec),
        scratch_shapes=[pltpu.VMEM((sub + SUBLANES, 2 * tn), F32)] * (tm // sub),
        compiler_params=_params("parallel", "arbitrary"),
        name="upconv_prompt",
    )(xn, w_up_p, w_conv_p, b_conv_p.reshape(1, 2 * dff))
    return act, jnp.concatenate([cna, cng], axis=-1)


def _upconv_sample_body(x_ref, w_ref, wc_ref, b_ref, pa_ref, pg_ref, act_ref, cna_ref, cng_ref):
    tn = act_ref.shape[1]
    u = _dot(x_ref[...], w_ref[...])
    um2 = jnp.concatenate([pa_ref[:, 0, :], pg_ref[:, 0, :]], axis=-1)
    um1 = jnp.concatenate([pa_ref[:, 1, :], pg_ref[:, 1, :]], axis=-1)
    cna_ref[:, 0, :] = pa_ref[:, 1, :]
    cna_ref[:, 1, :] = u[:, :tn]
    cng_ref[:, 0, :] = pg_ref[:, 1, :]
    cng_ref[:, 1, :] = u[:, tn:]
    act_ref[...] = _conv_gate(wc_ref[...], b_ref[...], um2, um1, u, tn).astype(act_ref.dtype)


def upconv_sample(xn, w_up_p, w_conv_p, b_conv_p, conv_prev, *, tn=UP_TN):
    m, d = xn.shape
    dff = w_up_p.shape[1] // 2
    nj = dff // tn
    cn = jax.ShapeDtypeStruct((m, CONV_W - 1, dff), F32)
    cn_spec = pl.BlockSpec((m, CONV_W - 1, tn), lambda j: (0, 0, j))
    act, cna, cng = pl.pallas_call(
        _upconv_sample_body,
        out_shape=(jax.ShapeDtypeStruct((m, dff), BF16), cn, cn),
        grid=(nj,),
        in_specs=[pl.BlockSpec((m, d), lambda j: (0, 0)),
                  pl.BlockSpec((d, 2 * tn), lambda j: (0, j)),
                  pl.BlockSpec((CONV_W, 2 * tn), lambda j: (0, j)),
                  pl.BlockSpec((1, 2 * tn), lambda j: (0, j)),
                  pl.BlockSpec((m, CONV_W - 1, tn), lambda j: (0, 0, j)),
                  pl.BlockSpec((m, CONV_W - 1, tn), lambda j: (0, 0, j + nj))],
        out_specs=(pl.BlockSpec((m, tn), lambda j: (0, j)), cn_spec, cn_spec),
        compiler_params=_params("parallel"),
        name="upconv_sample",
    )(xn, w_up_p, w_conv_p, b_conv_p.reshape(1, 2 * dff), conv_prev, conv_prev)
    return act, jnp.concatenate([cna, cng], axis=-1)


def _split3_bf16(x):
    hi = x.astype(BF16)
    r1 = x - hi.astype(F32)
    mid = r1.astype(BF16)
    lo = (r1 - mid.astype(F32)).astype(BF16)
    return hi, mid, lo


def _gla_prompt_body(zq_ref, zf_ref, zi_ref, zg_ref, bf_ref, lb_ref, gon_ref, o_ref, s_ref, st_ref):
    c = pl.program_id(1)
    n_chunks = pl.num_programs(1)
    C = zq_ref.shape[0]

    @pl.when(c == 0)
    def _():
        st_ref[...] = jnp.zeros(st_ref.shape, F32)

    lb = lb_ref[...]
    sig, nsig = _sigmoid_pair(zf_ref[...] + bf_ref[...])
    logf = jnp.log(lb + (1.0 - lb) * sig)
    key = (1.0 - lb) * nsig

    row = lax.broadcasted_iota(jnp.int32, (C, C), 0)
    col = lax.broadcasted_iota(jnp.int32, (C, C), 1)
    causal = row >= col
    tri = causal.astype(BF16)
    hi, mid, lo = _split3_bf16(logf)
    cum = (_dot(tri, lo) + _dot(tri, mid)) + _dot(tri, hi)

    last = cum[C - 1:C, :]
    q_dec = zq_ref[...] * jnp.exp(cum)
    k_inv = key * jnp.exp(-cum)
    k_last = key * jnp.exp(last - cum)
    decay = jnp.exp(last)
    gon = gon_ref[...]

    for h in range(N_HEADS):
        sl = slice(h * HEAD_DIM, (h + 1) * HEAD_DIM)
        qh = q_dec[:, sl].astype(BF16)
        vh = zi_ref[:, sl].astype(BF16)
        att = jnp.where(causal, _dot_nt(qh, k_inv[:, sl].astype(BF16)), 0.0)
        st = st_ref[h]
        o = _dot_nt(qh, st.astype(BF16)) + _dot(att.astype(BF16), vh)
        st_ref[h] = decay[:, sl] * st + _dot_tn(vh, k_last[:, sl].astype(BF16))
        ms = jnp.mean(o * o, axis=-1, keepdims=True)
        on = o * lax.rsqrt(ms + RMS_EPS) * gon
        o_ref[:, sl] = (on * _silu(zg_ref[:, sl])).astype(o_ref.dtype)

    @pl.when(c == n_chunks - 1)
    def _():
        for h in range(N_HEADS):
            s_ref[0, h] = st_ref[h].T


def hgrn2_prompt(z, seq_len, b_f, lb, g_onorm):
    m = z.shape[0]
    w = N_HEADS * HEAD_DIM
    n_seq = m // seq_len
    C = GLA_CHUNK if seq_len % GLA_CHUNK == 0 else seq_len
    nc = seq_len // C
    zspec = lambda g: pl.BlockSpec((C, w), lambda b, c: (b * nc + c, g))
    vec = pl.BlockSpec((1, w), lambda b, c: (0, 0))
    return pl.pallas_call(
        _gla_prompt_body,
        out_shape=(jax.ShapeDtypeStruct((m, w), BF16),
                   jax.ShapeDtypeStruct((n_seq, N_HEADS, HEAD_DIM, HEAD_DIM), F32)),
        grid=(n_seq, nc),
        in_specs=[zspec(0), zspec(1), zspec(2), zspec(3), vec, vec,
                  pl.BlockSpec((1, HEAD_DIM), lambda b, c: (0, 0))],
        out_specs=(pl.BlockSpec((C, w), lambda b, c: (b * nc + c, 0)),
                   pl.BlockSpec((1, N_HEADS, HEAD_DIM, HEAD_DIM), lambda b, c: (b, 0, 0, 0))),
        scratch_shapes=[pltpu.VMEM((N_HEADS, HEAD_DIM, HEAD_DIM), F32)],
        compiler_params=_params("parallel", "arbitrary"),
        name="hgrn2_prompt",
    )(z, z, z, z, b_f.reshape(1, w), lb.reshape(1, w), g_onorm.reshape(1, HEAD_DIM))


def _hgrn_sample_body(zqT_ref, zfT_ref, zi_ref, zg_ref, bfT_ref, lbT_ref, gon_ref, s_ref,
                      o_ref, so_ref):
    nb = zi_ref.shape[0]
    lb = lbT_ref[...]
    sig, nsig = _sigmoid_pair(zfT_ref[...] + bfT_ref[...])
    f = lb + (1.0 - lb) * sig
    key = (1.0 - lb) * nsig
    q = zqT_ref[...]
    v = zi_ref[...]
    rows = []
    for b in range(nb):
        s_new = f[:, b:b + 1] * s_ref[b, 0] + key[:, b:b + 1] * v[b:b + 1, :]
        so_ref[b, 0] = s_new
        rows.append(jnp.sum(q[:, b:b + 1] * s_new, axis=0, keepdims=True))
    o = jnp.concatenate(rows, axis=0)
    ms = jnp.mean(o * o, axis=-1, keepdims=True)
    on = o * lax.rsqrt(ms + RMS_EPS) * gon_ref[...]
    o_ref[...] = (on * _silu(zg_ref[...])).astype(o_ref.dtype)


def hgrn2_sample(z, b_f, lb, g_onorm, s0):
    nb = z.shape[0]
    w = N_HEADS * HEAD_DIM
    zt = z[:, :2 * w].T
    colspec = lambda g: pl.BlockSpec((HEAD_DIM, nb), lambda h: (g * N_HEADS + h, 0))
    rowspec = lambda g: pl.BlockSpec((nb, HEAD_DIM), lambda h: (0, g * N_HEADS + h))
    vecT = pl.BlockSpec((HEAD_DIM, 1), lambda h: (h, 0))
    sspec = pl.BlockSpec((nb, 1, HEAD_DIM, HEAD_DIM), lambda h: (0, h, 0, 0))
    return pl.pallas_call(
        _hgrn_sample_body,
        out_shape=(jax.ShapeDtypeStruct((nb, w), BF16), jax.ShapeDtypeStruct(s0.shape, F32)),
        grid=(N_HEADS,),
        in_specs=[colspec(0), colspec(1), rowspec(2), rowspec(3), vecT, vecT,
                  pl.BlockSpec((1, HEAD_DIM), lambda h: (0, 0)), sspec],
        out_specs=(pl.BlockSpec((nb, HEAD_DIM), lambda h: (0, h)), sspec),
        compiler_params=_params("parallel"),
        name="hgrn2_sample",
    )(zt, zt, z, z, b_f.reshape(w, 1), lb.reshape(w, 1), g_onorm.reshape(1, HEAD_DIM), s0)


PAGE_SUM_BLOCKS_PER_STEP = 4
PAGES_PER_BLOCK = MOBA_BLOCK // PAGE_ROWS


def _page_sums(page_refs, o_ref):
    for g in range(len(page_refs) // PAGES_PER_BLOCK):
        s = page_refs[g * PAGES_PER_BLOCK][0].sum(axis=0)
        for r in range(1, PAGES_PER_BLOCK):
            s = s + page_refs[g * PAGES_PER_BLOCK + r][0].sum(axis=0)
        for h in range(N_HEADS):
            o_ref[0, g, :, h * HEAD_DIM:(h + 1) * HEAD_DIM] = s[h:h + 1, :]


def _page_sum_specs(page_table, step_of):
    n_pages = page_table.shape[1]
    P = PAGE_SUM_BLOCKS_PER_STEP * PAGES_PER_BLOCK
    steps_per_seq = n_pages // P
    assert n_pages % P == 0

    def page_spec(p):
        def index(*args):
            ids, pt = args[:-1], args[-1]
            s = step_of(*ids)
            return (pt[s // steps_per_seq, (s % steps_per_seq) * P + p], 0, 0, 0)
        return pl.BlockSpec((1, PAGE_ROWS, N_HEADS, HEAD_DIM), index)

    def out_index(*args):
        s = step_of(*args[:-1])
        return (s // steps_per_seq, s % steps_per_seq, 0, 0)

    out_spec = pl.BlockSpec((1, PAGE_SUM_BLOCKS_PER_STEP, 1, N_HEADS * HEAD_DIM), out_index)
    return [page_spec(p) for p in range(P)], out_spec


def _key_sums_shape(page_table):
    nb, n_pages = page_table.shape
    return jax.ShapeDtypeStruct((nb, n_pages // PAGES_PER_BLOCK, 1, N_HEADS * HEAD_DIM), F32)


def _page_sum_body(pt_ref, *refs):
    _page_sums(refs[:-1], refs[-1])


def moba_block_key_sums(cache_k, page_table):
    nb, n_pages = page_table.shape
    steps_per_seq = n_pages // (PAGE_SUM_BLOCKS_PER_STEP * PAGES_PER_BLOCK)
    in_specs, out_spec = _page_sum_specs(page_table, lambda b, j: b * steps_per_seq + j)
    grid_spec = pltpu.PrefetchScalarGridSpec(
        num_scalar_prefetch=1, grid=(nb, steps_per_seq), in_specs=in_specs, out_specs=out_spec)
    return pl.pallas_call(
        _page_sum_body,
        out_shape=_key_sums_shape(page_table),
        grid_spec=grid_spec,
        compiler_params=_params("parallel", "parallel"),
        name="moba_page_sums",
    )(page_table, *([cache_k] * len(in_specs)))


MOBA_HEADS_PER_STEP = 2
KM_ROWS = 16


def _moba_prompt_body(pt_ref, slopes_ref, q_ref, k_ref, v_ref, *rest, n_blocks, n_page_refs):
    page_refs, rest = rest[:n_page_refs], rest[n_page_refs:]
    if n_page_refs:
        o_ref, ks_ref, kbf_ref, vbf_ref, kmh_ref, kml_ref = rest
        _page_sums(page_refs, ks_ref)
    else:
        o_ref, kbf_ref, vbf_ref, kmh_ref, kml_ref = rest
    hp = pl.program_id(1)
    qi = pl.program_id(2)
    T = MOBA_BLOCK
    G = MOBA_HEADS_PER_STEP
    scale = HEAD_DIM ** -0.5
    heads = [slice(g * HEAD_DIM, (g + 1) * HEAD_DIM) for g in range(G)]

    @pl.when(qi == 0)
    def _():
        k = k_ref[...]
        kbf_ref[...] = k.astype(BF16)
        vbf_ref[...] = v_ref[...].astype(BF16)
        sums = [jnp.sum(k[b * T:(b + 1) * T], axis=0, keepdims=True) for b in range(n_blocks)]
        sums.append(jnp.zeros((KM_ROWS - n_blocks, G * HEAD_DIM), F32))
        km = jnp.concatenate(sums, axis=0) * (1.0 / T)
        hi, lo = _split2_bf16(km)
        kmh_ref[...] = hi
        kml_ref[...] = lo

    blk_id = lax.broadcasted_iota(jnp.int32, (KM_ROWS, T), 0)

    def top_blocks(q, kmh, kml):
        qh, ql = _split2_bf16(q)
        sc = (_dot_nt(kmh, ql) + _dot_nt(kml, qh)) + _dot_nt(kmh, qh)
        sc = jnp.where(blk_id < qi, sc, -jnp.inf)
        picks = []
        for _ in range(MOBA_TOPK):
            mx = jnp.max(sc, axis=0, keepdims=True)
            hit = jnp.logical_and(sc == mx, mx > -jnp.inf)
            first = jnp.min(jnp.where(hit, blk_id, KM_ROWS), axis=0, keepdims=True)
            picks.append(first)
            sc = jnp.where(blk_id == first, -jnp.inf, sc)
        rows = jnp.concatenate(picks + [jnp.zeros((KM_ROWS - MOBA_TOPK, T), jnp.int32)], axis=0)
        rows = rows.astype(F32).astype(BF16)
        pick_row = lax.broadcasted_iota(jnp.int32, (KM_ROWS, LANES), 0)
        return tuple(_dot_tn(rows, (pick_row == t).astype(BF16)) for t in range(MOBA_TOPK))

    def all_past_blocks(q, kmh, kml):
        return tuple(jnp.full((T, LANES), t, F32) for t in range(MOBA_TOPK))

    ids = [lax.cond(qi > MOBA_TOPK, top_blocks, all_past_blocks,
                    q_ref[:, heads[g]], kmh_ref[:, heads[g]], kml_ref[:, heads[g]]) for g in range(G)]

    r = lax.broadcasted_iota(jnp.int32, (T, T), 0)
    cidx = lax.broadcasted_iota(jnp.int32, (T, T), 1)
    causal = r >= cidx
    rel = (r - cidx).astype(F32)
    own = pl.multiple_of(qi * T, T)
    qb = [q_ref[:, heads[g]].astype(BF16) for g in range(G)]
    bias = [rel * (-slopes_ref[hp * G + g]) for g in range(G)]

    carry = []
    for g in range(G):
        s = _dot_nt(qb[g], kbf_ref[pl.ds(own, T), heads[g]]) * scale + bias[g]
        s = jnp.where(causal, s, NEG_BIG)
        m0 = jnp.max(s, axis=-1, keepdims=True)
        p = jnp.exp(s - m0)
        carry += [m0, jnp.sum(p, axis=-1, keepdims=True),
                  _dot(p.astype(BF16), vbf_ref[pl.ds(own, T), heads[g]])]

    def past_block(i, carry):
        start = pl.multiple_of(i * T, T)
        far = ((qi - i) * T).astype(F32)
        blk = i.astype(F32)
        out = []
        for g in range(G):
            m, l, acc = carry[3 * g:3 * g + 3]
            chosen = functools.reduce(jnp.logical_or, [idg == blk for idg in ids[g]])
            shift = jnp.where(chosen, -slopes_ref[hp * G + g] * far, NEG_BIG)
            shift = jnp.concatenate([shift] * (T // LANES), axis=1)
            s = (_dot_nt(qb[g], kbf_ref[pl.ds(start, T), heads[g]]) * scale + bias[g]) + shift
            m_new = jnp.maximum(m, jnp.max(s, axis=-1, keepdims=True))
            alpha = jnp.exp(m - m_new)
            p = jnp.exp(s - m_new)
            out += [m_new, alpha * l + jnp.sum(p, axis=-1, keepdims=True),
                    alpha * acc + _dot(p.astype(BF16), vbf_ref[pl.ds(start, T), heads[g]])]
        return tuple(out)

    carry = lax.fori_loop(0, qi, past_block, tuple(carry))
    for g in range(G):
        m, l, acc = carry[3 * g:3 * g + 3]
        o_ref[:, heads[g]] = (acc / l).astype(o_ref.dtype)


def moba_prompt(z, seq_len, slopes, col0, cache_k, page_table):
    m = z.shape[0]
    n_seq = m // seq_len
    T = MOBA_BLOCK
    G = MOBA_HEADS_PER_STEP
    gw = G * HEAD_DIM
    assert seq_len % T == 0 and N_HEADS % G == 0 and col0 % G == 0
    nq = seq_len // T
    assert nq <= KM_ROWS
    c0 = col0 // G
    hg = N_HEADS // G
    n_sum_steps = page_table.size // (PAGE_SUM_BLOCKS_PER_STEP * PAGES_PER_BLOCK)
    ride_along = n_sum_steps == n_seq * hg * nq
    out_shape = [jax.ShapeDtypeStruct((m, N_HEADS * HEAD_DIM), BF16)]
    out_specs = [pl.BlockSpec((T, gw), lambda b, h, i, pt: (b * nq + i, h))]
    page_specs = []
    if ride_along:
        page_specs, sum_spec = _page_sum_specs(page_table, lambda b, h, i: (b * hg + h) * nq + i)
        out_shape.append(_key_sums_shape(page_table))
        out_specs.append(sum_spec)
    grid_spec = pltpu.PrefetchScalarGridSpec(
        num_scalar_prefetch=1,
        grid=(n_seq, hg, nq),
        in_specs=[pl.BlockSpec(memory_space=pltpu.SMEM),
                  pl.BlockSpec((T, gw), lambda b, h, i, pt: (b * nq + i, c0 + h)),
                  pl.BlockSpec((seq_len, gw), lambda b, h, i, pt: (b, c0 + hg + h)),
                  pl.BlockSpec((seq_len, gw), lambda b, h, i, pt: (b, c0 + 2 * hg + h))] + page_specs,
        out_specs=out_specs,
        scratch_shapes=[pltpu.VMEM((seq_len, gw), BF16), pltpu.VMEM((seq_len, gw), BF16),
                        pltpu.VMEM((KM_ROWS, gw), BF16), pltpu.VMEM((KM_ROWS, gw), BF16)],
    )
    outs = pl.pallas_call(
        functools.partial(_moba_prompt_body, n_blocks=nq, n_page_refs=len(page_specs)),
        out_shape=out_shape,
        grid_spec=grid_spec,
        compiler_params=_params("parallel", "parallel", "arbitrary"),
        name="moba_prompt",
    )(page_table, slopes, z, z, z, *([cache_k] * len(page_specs)))
    if ride_along:
        return outs[0], outs[1]
    return outs[0], moba_block_key_sums(cache_k, page_table)


def _sample_topk_body(ks_ref, q_ref, idx_ref, ok_ref):
    n_blk = ks_ref.shape[1]
    km = ks_ref[0, :, 0, :] * (1.0 / MOBA_BLOCK)
    prod = km * q_ref[0]
    lane = lax.broadcasted_iota(jnp.int32, (n_blk, LANES), 1)
    blk = lax.broadcasted_iota(jnp.int32, (n_blk, LANES), 0)
    sc = jnp.full((n_blk, LANES), -jnp.inf, F32)
    for h in range(N_HEADS):
        col = jnp.sum(prod[:, h * HEAD_DIM:(h + 1) * HEAD_DIM], axis=-1, keepdims=True)
        sc = jnp.where(lane == h, col, sc)
    idxs, oks = [], []
    for _ in range(MOBA_TOPK):
        mx = jnp.max(sc, axis=0, keepdims=True)
        first = jnp.min(jnp.where(sc == mx, blk, n_blk), axis=0, keepdims=True)
        idxs.append(first)
        oks.append(jnp.logical_and(mx > -jnp.inf, mx < jnp.inf).astype(jnp.int32))
        sc = jnp.where(blk == first, -jnp.inf, sc)
    idx_ref[0] = jnp.concatenate(idxs, axis=0)
    ok_ref[0] = jnp.concatenate(oks, axis=0)


def moba_sample_topk(key_sums, q3):
    nb, n_blk, _, w = key_sums.shape
    assert n_blk >= MOBA_TOPK
    out = jax.ShapeDtypeStruct((nb, MOBA_TOPK, LANES), jnp.int32)
    ospec = pl.BlockSpec((1, MOBA_TOPK, LANES), lambda b: (b, 0, 0))
    return pl.pallas_call(
        _sample_topk_body,
        out_shape=(out, out),
        grid=(nb,),
        in_specs=[pl.BlockSpec((1, n_blk, 1, w), lambda b: (b, 0, 0, 0)),
                  pl.BlockSpec((1, 1, w), lambda b: (b, 0, 0))],
        out_specs=(ospec, ospec),
        compiler_params=_params("parallel"),
        name="moba_sample_topk",
    )(key_sums, q3)


def _sample_attn_body(pt_ref, idx_ref, ok_ref, q_ref, kn_ref, vn_ref, ck_ref, cv_ref, o_ref,
                      kbuf, vbuf, sem, *, past_len):
    b = pl.program_id(0)
    T = MOBA_BLOCK
    pages_per_blk = T // PAGE_ROWS
    n_sel = MOBA_TOPK
    scale = HEAD_DIM ** -0.5

    def copies(h):
        out = []
        for j in range(n_sel):
            blk = idx_ref[b, j * N_HEADS + h]
            for r in range(pages_per_blk):
                page = pt_ref[b, blk * pages_per_blk + r]
                dst = pl.ds((j * pages_per_blk + r) * PAGE_ROWS, PAGE_ROWS)
                out.append(pltpu.make_async_copy(ck_ref.at[page, :, h, :], kbuf.at[h, dst, :], sem.at[0, h]))
                out.append(pltpu.make_async_copy(cv_ref.at[page, :, h, :], vbuf.at[h, dst, :], sem.at[1, h]))
        return out

    for h in range(N_HEADS):
        for cp in copies(h):
            cp.start()

    q = q_ref[0]
    kn = kn_ref[0]
    vn = vn_ref[0]
    colid = lax.broadcasted_iota(jnp.int32, (SUBLANES, n_sel * T), 1)
    for h in range(N_HEADS):
        for cp in copies(h):
            cp.wait()
        sl = slice(h * HEAD_DIM, (h + 1) * HEAD_DIM)
        slope = 2.0 ** (-8.0 * (h + 1) / N_HEADS)
        qh = q[:, sl]
        q8 = jnp.broadcast_to(qh, (SUBLANES, HEAD_DIM)).astype(BF16)
        s = _dot_nt(q8, kbuf[h].astype(BF16)) * scale
        dist = jnp.zeros(s.shape, F32)
        valid = jnp.zeros(s.shape, jnp.bool_)
        for j in range(n_sel):
            in_j = jnp.logical_and(colid >= j * T, colid < (j + 1) * T)
            key_pos = idx_ref[b, j * N_HEADS + h] * T - j * T + colid
            dist = jnp.where(in_j, (past_len - key_pos).astype(F32), dist)
            valid = jnp.logical_or(valid, jnp.logical_and(in_j, ok_ref[b, j * N_HEADS + h] > 0))
        s = jnp.where(valid, s - slope * dist, NEG_BIG)
        s_new = jnp.sum(qh * kn[:, sl], axis=-1, keepdims=True) * scale
        m = jnp.maximum(jnp.max(s, axis=-1, keepdims=True), s_new)
        p = jnp.exp(s - m)
        p_new = jnp.exp(s_new - m)
        l = jnp.sum(p, axis=-1, keepdims=True) + p_new
        acc = _dot(p.astype(BF16), vbuf[h].astype(BF16)) + p_new * vn[:, sl]
        o_ref[0, :, sl] = (acc[0:1] / l[0:1]).astype(o_ref.dtype)


def moba_sample_attend(q3, z3, cache_k, cache_v, page_table, idx, ok, kcol, past_len):
    nb = z3.shape[0]
    w = N_HEADS * HEAD_DIM
    rows = MOBA_TOPK * MOBA_BLOCK
    zspec = lambda g: pl.BlockSpec((1, 1, w), lambda b, *_: (b, 0, kcol + g))
    grid_spec = pltpu.PrefetchScalarGridSpec(
        num_scalar_prefetch=3,
        grid=(nb,),
        in_specs=[pl.BlockSpec((1, 1, w), lambda b, *_: (b, 0, 0)), zspec(0), zspec(1),
                  pl.BlockSpec(memory_space=pl.ANY), pl.BlockSpec(memory_space=pl.ANY)],
        out_specs=pl.BlockSpec((1, 1, w), lambda b, *_: (b, 0, 0)),
        scratch_shapes=[pltpu.VMEM((N_HEADS, rows, HEAD_DIM), F32),
                        pltpu.VMEM((N_HEADS, rows, HEAD_DIM), F32),
                        pltpu.SemaphoreType.DMA((2, N_HEADS))],
    )
    out = pl.pallas_call(
        functools.partial(_sample_attn_body, past_len=past_len),
        out_shape=jax.ShapeDtypeStruct((nb, 1, w), BF16),
        grid_spec=grid_spec,
        compiler_params=_params("arbitrary"),
        name="moba_sample_attend",
    )(page_table, idx, ok, q3, z3, z3, cache_k, cache_v)
    return out.reshape(nb, w)


def _alibi_slopes():
    return 2.0 ** (-8.0 * jnp.arange(1, N_HEADS + 1, dtype=F32) / N_HEADS)


def kernel(x_prompt, x_sample, p_prompt, p_sample, cache_k, cache_v, page_table, state_hgrn,
           state_ffn_conv, lb_logits, w_in, b_f, g_onorm, w_out, g_n1, g_n2, w_up, w_conv, b_conv,
           w_down, w_ple, w_pg, g_pn, g_final):
    depth = w_in.shape[0]
    assert depth == 1
    B, L, D = x_prompt.shape
    Bd, Ld, _ = x_sample.shape
    assert Ld == 1
    hgw = N_HEADS * HEAD_DIM
    n_pages = page_table.shape[1]
    past_len = n_pages * PAGE_ROWS
    moba_col128 = 4 * N_HEADS
    moba_colw = 4

    lbs = jnp.cumsum(jax.nn.softmax(lb_logits.astype(F32), axis=0), axis=0)
    slopes = _alibi_slopes()
    i = 0
    lb = lbs[i]
    win, wout, wdown, wpg, wple = (a[i].astype(BF16) for a in (w_in, w_out, w_down, w_pg, w_ple))
    wup = pair_up_columns(w_up[i]).astype(BF16)
    wconv, bconv = pair_up_columns(w_conv[i]), pair_up_columns(b_conv[i])

    def tail(h0, attn_cat, pe, upconv):
        h1 = matmul_residual(attn_cat, wout, h0, name="out_proj")
        act, conv_new = upconv(rmsnorm_rows(h1, g_n2[i], BF16))
        h2 = matmul_residual(act, wdown, h1, tn=256, single_buffer_a=True, name="down_proj")
        h3 = matmul_gate(rmsnorm_rows(h2, g_pn[i], BF16), wpg, h2, pe.astype(BF16), wple)
        return rmsnorm_rows(h3, g_final, F32), conv_new

    xp = x_prompt.reshape(B * L, D)
    zp = matmul(rmsnorm_rows(xp, g_n1[i], BF16), win, name="in_proj")
    o_hg, s_prompt = hgrn2_prompt(zp, L, b_f[i], lb, g_onorm[i])
    o_mb, key_sums = moba_prompt(zp, L, slopes, moba_col128, cache_k[i], page_table)
    y_p, conv_p = tail(xp, jnp.concatenate([o_hg, o_mb], axis=-1), p_prompt[i].reshape(B * L, -1),
                       lambda xn: upconv_prompt(xn, wup, wconv, bconv, L))
    k_p = zp[:, 5 * hgw:6 * hgw].reshape(1, B, L, N_HEADS, HEAD_DIM)
    v_p = zp[:, 6 * hgw:7 * hgw].reshape(1, B, L, N_HEADS, HEAD_DIM)

    xs = x_sample.reshape(Bd, D)
    xs_n = rmsnorm_rows(xs, g_n1[i], F32)
    zs = matmul(xs_n.astype(BF16), win, name="in_proj_s")
    o_hg_s, s_sample = hgrn2_sample(zs, b_f[i], lb, g_onorm[i], state_hgrn[i])
    zs3 = zs.reshape(Bd, 1, -1)
    qs3 = matmul_f32_cols(xs_n, w_in[i], moba_colw * hgw, hgw, name="in_proj_s_q").reshape(Bd, 1, hgw)
    idx, ok = moba_sample_topk(key_sums, qs3)
    o_mb_s = moba_sample_attend(qs3, zs3, cache_k[i], cache_v[i], page_table,
                                idx[:, :, :N_HEADS].reshape(Bd, -1), ok[:, :, :N_HEADS].reshape(Bd, -1),
                                moba_colw + 1, past_len)
    y_s, conv_s = tail(xs, jnp.concatenate([o_hg_s, o_mb_s], axis=-1), p_sample[i].reshape(Bd, -1),
                       lambda xn: upconv_sample(xn, wup, wconv, bconv, state_ffn_conv[i]))
    k_s = zs[:, 5 * hgw:6 * hgw].reshape(1, Bd, 1, N_HEADS, HEAD_DIM)
    v_s = zs[:, 6 * hgw:7 * hgw].reshape(1, Bd, 1, N_HEADS, HEAD_DIM)

    return (y_p.reshape(B, L, D), y_s.reshape(Bd, 1, D), k_p, v_p, s_prompt[None], conv_p[None],
            k_s, v_s, s_sample[None], conv_s[None])
```

```python
import functools

import jax
import jax.numpy as jnp
from jax import lax
from jax.experimental import pallas as pl
from jax.experimental.pallas import tpu as pltpu

F32 = jnp.float32
BF16 = jnp.bfloat16

HEAD_DIM = 128
N_HEADS = 16
MOBA_BLOCK = 256
MOBA_TOPK = 3
GLA_CHUNK = 64
CONV_W = 3
PAGE_ROWS = 128
RMS_EPS = 1e-6
NEG_BIG = -1e30

V7X_VMEM_LIMIT_BYTES = 56 * 1024 * 1024
SUBLANES = 8
LANES = 128


def _params(*sem):
    return pltpu.CompilerParams(dimension_semantics=sem, vmem_limit_bytes=V7X_VMEM_LIMIT_BYTES)


def _dot(a, b):
    return jnp.dot(a, b, preferred_element_type=F32)


def _dot_nt(a, b):
    return lax.dot_general(a, b, (((1,), (1,)), ((), ())), preferred_element_type=F32)


def _dot_tn(a, b):
    return lax.dot_general(a, b, (((0,), (0,)), ((), ())), preferred_element_type=F32)


def _sigmoid_pair(x):
    e = jnp.exp(-jnp.abs(x))
    r = 1.0 / (1.0 + e)
    er = e * r
    pos = x >= 0
    return jnp.where(pos, r, er), jnp.where(pos, er, r)


def _silu(x):
    return x * _sigmoid_pair(x)[0]


def _rms_body(x_ref, g_ref, o_ref):
    x = x_ref[...]
    ms = jnp.mean(x * x, axis=-1, keepdims=True)
    o_ref[...] = (x * lax.rsqrt(ms + RMS_EPS) * g_ref[...]).astype(o_ref.dtype)


def rmsnorm_rows(x, g, out_dtype):
    m, d = x.shape
    tm = min(m, 256)
    return pl.pallas_call(
        _rms_body,
        out_shape=jax.ShapeDtypeStruct((m, d), out_dtype),
        grid=(m // tm,),
        in_specs=[pl.BlockSpec((tm, d), lambda i: (i, 0)), pl.BlockSpec((1, d), lambda i: (0, 0))],
        out_specs=pl.BlockSpec((tm, d), lambda i: (i, 0)),
        compiler_params=_params("parallel"),
        name="rmsnorm",
    )(x, g.reshape(1, d))


def _mm_plain_body(a_ref, w_ref, o_ref):
    o_ref[...] = _dot(a_ref[...], w_ref[...])


def _mm_res_body(a_ref, w_ref, r_ref, o_ref):
    o_ref[...] = r_ref[...] + _dot(a_ref[...], w_ref[...])


def _mm_gate_body(a_ref, w_ref, h_ref, pe_ref, wple_ref, o_ref):
    gate = _sigmoid_pair(_dot(a_ref[...], w_ref[...]))[0]
    o_ref[...] = h_ref[...] + gate * _dot(pe_ref[...], wple_ref[...])


def _mm_call(body, a, w, extra, extra_specs, tm, tn, single_buffer_a, name):
    m, k = a.shape
    n = w.shape[1]
    tm = min(tm, m)
    a_kw = dict(pipeline_mode=pl.Buffered(1)) if single_buffer_a else {}
    return pl.pallas_call(
        body,
        out_shape=jax.ShapeDtypeStruct((m, n), F32),
        grid=(m // tm, n // tn),
        in_specs=[pl.BlockSpec((tm, k), lambda i, j: (i, 0), **a_kw),
                  pl.BlockSpec((k, tn), lambda i, j: (0, j))] + extra_specs(tm, tn),
        out_specs=pl.BlockSpec((tm, tn), lambda i, j: (i, j)),
        compiler_params=_params("parallel", "arbitrary"),
        name=name,
    )(a, w, *extra)


def matmul(a, w, *, tm=1024, tn=1024, name="matmul"):
    return _mm_call(_mm_plain_body, a, w, (), lambda tm, tn: [], tm, tn, False, name)


def matmul_residual(a, w, res, *, tm=1024, tn=512, single_buffer_a=False, name="matmul_residual"):
    specs = lambda tm, tn: [pl.BlockSpec((tm, tn), lambda i, j: (i, j))]
    return _mm_call(_mm_res_body, a, w, (res,), specs, tm, tn, single_buffer_a, name)


def matmul_gate(a, w, h, pe, wple, *, tm=1024, tn=512, name="matmul_gate"):
    kp = pe.shape[1]
    specs = lambda tm, tn: [pl.BlockSpec((tm, tn), lambda i, j: (i, j)),
                            pl.BlockSpec((tm, kp), lambda i, j: (i, 0)),
                            pl.BlockSpec((kp, tn), lambda i, j: (0, j))]
    return _mm_call(_mm_gate_body, a, w, (h, pe, wple), specs, tm, tn, False, name)


def _split2_bf16(x):
    hi = x.astype(BF16)
    return hi, (x - hi.astype(F32)).astype(BF16)


def _mm_x3_body(a_ref, w_ref, o_ref):
    ah, al = _split2_bf16(a_ref[...])
    wh, wl = _split2_bf16(w_ref[...])
    o_ref[...] = (_dot(al, wh) + _dot(ah, wl)) + _dot(ah, wh)


def matmul_f32_cols(a, w, col0, n, *, tn=512, name="matmul_x3"):
    m, k = a.shape
    assert col0 % tn == 0 and n % tn == 0
    return pl.pallas_call(
        _mm_x3_body,
        out_shape=jax.ShapeDtypeStruct((m, n), F32),
        grid=(n // tn,),
        in_specs=[pl.BlockSpec((m, k), lambda j: (0, 0)),
                  pl.BlockSpec((k, tn), lambda j: (0, col0 // tn + j))],
        out_specs=pl.BlockSpec((m, tn), lambda j: (0, j)),
        compiler_params=_params("parallel"),
        name=name,
    )(a, w)


UP_TN = 256


def pair_up_columns(a, tn=UP_TN):
    lead, n = a.shape[:-1], a.shape[-1]
    nj = n // (2 * tn)
    return a.reshape(lead + (2, nj, tn)).swapaxes(-3, -2).reshape(lead + (n,))


def _paired_weight(wa_ref, wg_ref):
    return jnp.concatenate([wa_ref[...], wg_ref[...]], axis=1)


def _conv_gate(wc, b, um2, um1, u, tn):
    c = b + wc[0:1] * um2 + wc[1:2] * um1 + wc[2:3] * u
    return _silu(c[:, tn:]) * c[:, :tn]


def _upconv_prompt_body(x_ref, wa_ref, wg_ref, wc_ref, b_ref, act_ref, cna_ref, cng_ref, *u_refs):
    tm = x_ref.shape[0]
    tn = act_ref.shape[1]
    n_sub = len(u_refs)
    sub = tm // n_sub
    H = SUBLANES
    u_refs[0][0:H, :] = jnp.zeros((H, 2 * tn), F32)
    w = _paired_weight(wa_ref, wg_ref)
    wc = wc_ref[...]
    b = b_ref[...]

    def project(r):
        u = _dot(x_ref[r * sub:(r + 1) * sub, :], w)
        u_refs[r][H:, :] = u
        if r + 1 < n_sub:
            u_refs[r + 1][0:H, :] = u[sub - H:, :]
        else:
            cna_ref[0] = u[sub - (CONV_W - 1):, 0:tn]
            cng_ref[0] = u[sub - (CONV_W - 1):, tn:]

    def conv_gate(r):
        u_ref = u_refs[r]
        act = _conv_gate(wc, b, u_ref[H - 2:H - 2 + sub, :], u_ref[H - 1:H - 1 + sub, :],
                         u_ref[H:H + sub, :], tn)
        act_ref[r * sub:(r + 1) * sub, :] = act.astype(act_ref.dtype)

    project(0)
    for r in range(n_sub):
        if r + 1 < n_sub:
            project(r + 1)
        conv_gate(r)


def upconv_prompt(xn, w_up, w_conv_p, b_conv_p, seq_len, *, tn=UP_TN, sub=512):
    m, d = xn.shape
    dff = w_up.shape[1] // 2
    tm = seq_len
    sub = min(sub, tm)
    nj = dff // tn
    n_seq = m // seq_len
    cn = jax.ShapeDtypeStruct((n_seq, CONV_W - 1, dff), F32)
    cn_spec = pl.BlockSpec((1, CONV_W - 1, tn), lambda i, j: (i, 0, j))
    act, cna, cng = pl.pallas_call(
        _upconv_prompt_body,
        out_shape=(jax.ShapeDtypeStruct((m, dff), BF16), cn, cn),
        grid=(n_seq, nj),
        in_specs=[pl.BlockSpec((tm, d), lambda i, j: (i, 0), pipeline_mode=pl.Buffered(1)),
                  pl.BlockSpec((d, tn), lambda i, j: (0, j)),
                  pl.BlockSpec((d, tn), lambda i, j: (0, j + nj)),
                  pl.BlockSpec((CONV_W, 2 * tn), lambda i, j: (0, j)),
                  pl.BlockSpec((1, 2 * tn), lambda i, j: (0, j))],
        out_specs=(pl.BlockSpec((tm, tn), lambda i, j: (i, j)), cn_spec, cn_spec),
        scratch_shapes=[pltpu.VMEM((sub + SUBLANES, 2 * tn), F32)] * (tm // sub),
        compiler_params=_params("parallel", "arbitrary"),
        name="upconv_prompt",
    )(xn, w_up, w_up, w_conv_p, b_conv_p.reshape(1, 2 * dff))
    return act, jnp.concatenate([cna, cng], axis=-1)


def _upconv_sample_body(x_ref, wa_ref, wg_ref, wc_ref, b_ref, pa_ref, pg_ref, act_ref, cna_ref, cng_ref):
    tn = act_ref.shape[1]
    u = _dot(x_ref[...], _paired_weight(wa_ref, wg_ref))
    um2 = jnp.concatenate([pa_ref[:, 0, :], pg_ref[:, 0, :]], axis=-1)
    um1 = jnp.concatenate([pa_ref[:, 1, :], pg_ref[:, 1, :]], axis=-1)
    cna_ref[:, 0, :] = pa_ref[:, 1, :]
    cna_ref[:, 1, :] = u[:, :tn]
    cng_ref[:, 0, :] = pg_ref[:, 1, :]
    cng_ref[:, 1, :] = u[:, tn:]
    act_ref[...] = _conv_gate(wc_ref[...], b_ref[...], um2, um1, u, tn).astype(act_ref.dtype)


def upconv_sample(xn, w_up, w_conv_p, b_conv_p, conv_prev, *, tn=UP_TN):
    m, d = xn.shape
    dff = w_up.shape[1] // 2
    nj = dff // tn
    cn = jax.ShapeDtypeStruct((m, CONV_W - 1, dff), F32)
    cn_spec = pl.BlockSpec((m, CONV_W - 1, tn), lambda j: (0, 0, j))
    act, cna, cng = pl.pallas_call(
        _upconv_sample_body,
        out_shape=(jax.ShapeDtypeStruct((m, dff), BF16), cn, cn),
        grid=(nj,),
        in_specs=[pl.BlockSpec((m, d), lambda j: (0, 0)),
                  pl.BlockSpec((d, tn), lambda j: (0, j)),
                  pl.BlockSpec((d, tn), lambda j: (0, j + nj)),
                  pl.BlockSpec((CONV_W, 2 * tn), lambda j: (0, j)),
                  pl.BlockSpec((1, 2 * tn), lambda j: (0, j)),
                  pl.BlockSpec((m, CONV_W - 1, tn), lambda j: (0, 0, j)),
                  pl.BlockSpec((m, CONV_W - 1, tn), lambda j: (0, 0, j + nj))],
        out_specs=(pl.BlockSpec((m, tn), lambda j: (0, j)), cn_spec, cn_spec),
        compiler_params=_params("parallel"),
        name="upconv_sample",
    )(xn, w_up, w_up, w_conv_p, b_conv_p.reshape(1, 2 * dff), conv_prev, conv_prev)
    return act, jnp.concatenate([cna, cng], axis=-1)


def _split3_bf16(x):
    hi = x.astype(BF16)
    r1 = x - hi.astype(F32)
    mid = r1.astype(BF16)
    lo = (r1 - mid.astype(F32)).astype(BF16)
    return hi, mid, lo


def _gla_prompt_body(zq_ref, zf_ref, zi_ref, zg_ref, bf_ref, lb_ref, gon_ref, o_ref, s_ref, st_ref):
    c = pl.program_id(1)
    n_chunks = pl.num_programs(1)
    C = zq_ref.shape[0]

    @pl.when(c == 0)
    def _():
        st_ref[...] = jnp.zeros(st_ref.shape, F32)

    lb = lb_ref[...]
    sig, nsig = _sigmoid_pair(zf_ref[...] + bf_ref[...])
    logf = jnp.log(lb + (1.0 - lb) * sig)
    key = (1.0 - lb) * nsig

    row = lax.broadcasted_iota(jnp.int32, (C, C), 0)
    col = lax.broadcasted_iota(jnp.int32, (C, C), 1)
    causal = row >= col
    tri = causal.astype(BF16)
    hi, mid, lo = _split3_bf16(logf)
    cum = (_dot(tri, lo) + _dot(tri, mid)) + _dot(tri, hi)

    last = cum[C - 1:C, :]
    q_dec = zq_ref[...] * jnp.exp(cum)
    k_inv = key * jnp.exp(-cum)
    k_last = key * jnp.exp(last - cum)
    decay = jnp.exp(last)
    gon = gon_ref[...]

    for h in range(N_HEADS):
        sl = slice(h * HEAD_DIM, (h + 1) * HEAD_DIM)
        qh = q_dec[:, sl].astype(BF16)
        vh = zi_ref[:, sl].astype(BF16)
        att = jnp.where(causal, _dot_nt(qh, k_inv[:, sl].astype(BF16)), 0.0)
        st = st_ref[h]
        o = _dot_nt(qh, st.astype(BF16)) + _dot(att.astype(BF16), vh)
        st_ref[h] = decay[:, sl] * st + _dot_tn(vh, k_last[:, sl].astype(BF16))
        ms = jnp.mean(o * o, axis=-1, keepdims=True)
        on = o * lax.rsqrt(ms + RMS_EPS) * gon
        o_ref[:, sl] = (on * _silu(zg_ref[:, sl])).astype(o_ref.dtype)

    @pl.when(c == n_chunks - 1)
    def _():
        for h in range(N_HEADS):
            s_ref[0, h] = st_ref[h].T


def hgrn2_prompt(z, seq_len, b_f, lb, g_onorm):
    m = z.shape[0]
    w = N_HEADS * HEAD_DIM
    n_seq = m // seq_len
    C = GLA_CHUNK if seq_len % GLA_CHUNK == 0 else seq_len
    nc = seq_len // C
    zspec = lambda g: pl.BlockSpec((C, w), lambda b, c: (b * nc + c, g))
    vec = pl.BlockSpec((1, w), lambda b, c: (0, 0))
    return pl.pallas_call(
        _gla_prompt_body,
        out_shape=(jax.ShapeDtypeStruct((m, w), BF16),
                   jax.ShapeDtypeStruct((n_seq, N_HEADS, HEAD_DIM, HEAD_DIM), F32)),
        grid=(n_seq, nc),
        in_specs=[zspec(0), zspec(1), zspec(2), zspec(3), vec, vec,
                  pl.BlockSpec((1, HEAD_DIM), lambda b, c: (0, 0))],
        out_specs=(pl.BlockSpec((C, w), lambda b, c: (b * nc + c, 0)),
                   pl.BlockSpec((1, N_HEADS, HEAD_DIM, HEAD_DIM), lambda b, c: (b, 0, 0, 0))),
        scratch_shapes=[pltpu.VMEM((N_HEADS, HEAD_DIM, HEAD_DIM), F32)],
        compiler_params=_params("parallel", "arbitrary"),
        name="hgrn2_prompt",
    )(z, z, z, z, b_f.reshape(1, w), lb.reshape(1, w), g_onorm.reshape(1, HEAD_DIM))


def _hgrn_sample_body(zqT_ref, zfT_ref, zi_ref, zg_ref, bfT_ref, lbT_ref, gon_ref, s_ref,
                      o_ref, so_ref):
    nb = zi_ref.shape[0]
    lb = lbT_ref[...]
    sig, nsig = _sigmoid_pair(zfT_ref[...] + bfT_ref[...])
    f = lb + (1.0 - lb) * sig
    key = (1.0 - lb) * nsig
    q = zqT_ref[...]
    v = zi_ref[...]
    rows = []
    for b in range(nb):
        s_new = f[:, b:b + 1] * s_ref[b, 0] + key[:, b:b + 1] * v[b:b + 1, :]
        so_ref[b, 0] = s_new
        rows.append(jnp.sum(q[:, b:b + 1] * s_new, axis=0, keepdims=True))
    o = jnp.concatenate(rows, axis=0)
    ms = jnp.mean(o * o, axis=-1, keepdims=True)
    on = o * lax.rsqrt(ms + RMS_EPS) * gon_ref[...]
    o_ref[...] = (on * _silu(zg_ref[...])).astype(o_ref.dtype)


def hgrn2_sample(z, b_f, lb, g_onorm, s0):
    nb = z.shape[0]
    w = N_HEADS * HEAD_DIM
    zt = z[:, :2 * w].T
    colspec = lambda g: pl.BlockSpec((HEAD_DIM, nb), lambda h: (g * N_HEADS + h, 0))
    rowspec = lambda g: pl.BlockSpec((nb, HEAD_DIM), lambda h: (0, g * N_HEADS + h))
    vecT = pl.BlockSpec((HEAD_DIM, 1), lambda h: (h, 0))
    sspec = pl.BlockSpec((nb, 1, HEAD_DIM, HEAD_DIM), lambda h: (0, h, 0, 0))
    return pl.pallas_call(
        _hgrn_sample_body,
        out_shape=(jax.ShapeDtypeStruct((nb, w), BF16), jax.ShapeDtypeStruct(s0.shape, F32)),
        grid=(N_HEADS,),
        in_specs=[colspec(0), colspec(1), rowspec(2), rowspec(3), vecT, vecT,
                  pl.BlockSpec((1, HEAD_DIM), lambda h: (0, 0)), sspec],
        out_specs=(pl.BlockSpec((nb, HEAD_DIM), lambda h: (0, h)), sspec),
        compiler_params=_params("parallel"),
        name="hgrn2_sample",
    )(zt, zt, z, z, b_f.reshape(w, 1), lb.reshape(w, 1), g_onorm.reshape(1, HEAD_DIM), s0)


PAGE_SUM_BLOCKS_PER_STEP = 4
PAGES_PER_BLOCK = MOBA_BLOCK // PAGE_ROWS


def _page_sums(page_refs, o_ref):
    for g in range(len(page_refs) // PAGES_PER_BLOCK):
        s = page_refs[g * PAGES_PER_BLOCK][0].sum(axis=0)
        for r in range(1, PAGES_PER_BLOCK):
            s = s + page_refs[g * PAGES_PER_BLOCK + r][0].sum(axis=0)
        for h in range(N_HEADS):
            o_ref[0, g, :, h * HEAD_DIM:(h + 1) * HEAD_DIM] = s[h:h + 1, :]


def _page_sum_specs(page_table, step_of):
    n_pages = page_table.shape[1]
    P = PAGE_SUM_BLOCKS_PER_STEP * PAGES_PER_BLOCK
    steps_per_seq = n_pages // P
    assert n_pages % P == 0

    def page_spec(p):
        def index(*args):
            ids, pt = args[:-1], args[-1]
            s = step_of(*ids)
            return (pt[s // steps_per_seq, (s % steps_per_seq) * P + p], 0, 0, 0)
        return pl.BlockSpec((1, PAGE_ROWS, N_HEADS, HEAD_DIM), index)

    def out_index(*args):
        s = step_of(*args[:-1])
        return (s // steps_per_seq, s % steps_per_seq, 0, 0)

    out_spec = pl.BlockSpec((1, PAGE_SUM_BLOCKS_PER_STEP, 1, N_HEADS * HEAD_DIM), out_index)
    return [page_spec(p) for p in range(P)], out_spec


def _key_sums_shape(page_table):
    nb, n_pages = page_table.shape
    return jax.ShapeDtypeStruct((nb, n_pages // PAGES_PER_BLOCK, 1, N_HEADS * HEAD_DIM), F32)


def _page_sum_body(pt_ref, *refs):
    _page_sums(refs[:-1], refs[-1])


def moba_block_key_sums(cache_k, page_table):
    nb, n_pages = page_table.shape
    steps_per_seq = n_pages // (PAGE_SUM_BLOCKS_PER_STEP * PAGES_PER_BLOCK)
    in_specs, out_spec = _page_sum_specs(page_table, lambda b, j: b * steps_per_seq + j)
    grid_spec = pltpu.PrefetchScalarGridSpec(
        num_scalar_prefetch=1, grid=(nb, steps_per_seq), in_specs=in_specs, out_specs=out_spec)
    return pl.pallas_call(
        _page_sum_body,
        out_shape=_key_sums_shape(page_table),
        grid_spec=grid_spec,
        compiler_params=_params("parallel", "parallel"),
        name="moba_page_sums",
    )(page_table, *([cache_k] * len(in_specs)))


MOBA_HEADS_PER_STEP = 2
KM_ROWS = 16


def _moba_prompt_body(pt_ref, slopes_ref, q_ref, k_ref, v_ref, *rest, n_blocks, n_page_refs):
    page_refs, rest = rest[:n_page_refs], rest[n_page_refs:]
    if n_page_refs:
        o_ref, ks_ref, kbf_ref, vbf_ref, kmh_ref, kml_ref, s_ref = rest
        _page_sums(page_refs, ks_ref)
    else:
        o_ref, kbf_ref, vbf_ref, kmh_ref, kml_ref, s_ref = rest
    hp = pl.program_id(1)
    qi = pl.program_id(2)
    T = MOBA_BLOCK
    G = MOBA_HEADS_PER_STEP
    scale = HEAD_DIM ** -0.5
    heads = [slice(g * HEAD_DIM, (g + 1) * HEAD_DIM) for g in range(G)]

    @pl.when(qi == 0)
    def _():
        k = k_ref[...]
        kbf_ref[...] = k.astype(BF16)
        vbf_ref[...] = v_ref[...].astype(BF16)
        sums = [jnp.sum(k[b * T:(b + 1) * T], axis=0, keepdims=True) for b in range(n_blocks)]
        sums.append(jnp.zeros((KM_ROWS - n_blocks, G * HEAD_DIM), F32))
        km = jnp.concatenate(sums, axis=0) * (1.0 / T)
        hi, lo = _split2_bf16(km)
        kmh_ref[...] = hi
        kml_ref[...] = lo

    blk_id = lax.broadcasted_iota(jnp.int32, (KM_ROWS, T), 0)

    def top_blocks(q, kmh, kml, n_past):
        qh, ql = _split2_bf16(q)
        sc = (_dot_nt(kmh, ql) + _dot_nt(kml, qh)) + _dot_nt(kmh, qh)
        sc = jnp.where(blk_id < n_past, sc, -jnp.inf)
        picks = []
        for _ in range(MOBA_TOPK):
            mx = jnp.max(sc, axis=0, keepdims=True)
            hit = jnp.logical_and(sc == mx, mx > -jnp.inf)
            first = jnp.min(jnp.where(hit, blk_id, KM_ROWS), axis=0, keepdims=True)
            picks.append(first)
            sc = jnp.where(blk_id == first, -jnp.inf, sc)
        rows = jnp.concatenate(picks + [jnp.zeros((KM_ROWS - MOBA_TOPK, T), jnp.int32)], axis=0)
        rows = rows.astype(F32).astype(BF16)
        pick_row = lax.broadcasted_iota(jnp.int32, (KM_ROWS, LANES), 0)
        return [_dot_tn(rows, (pick_row == t).astype(BF16)) for t in range(MOBA_TOPK)]

    r = lax.broadcasted_iota(jnp.int32, (T, T), 0)
    cidx = lax.broadcasted_iota(jnp.int32, (T, T), 1)
    causal = r >= cidx
    rel = (r - cidx).astype(F32)

    def attend(g, n):
        q = q_ref[:, heads[g]]
        qb = q.astype(BF16)
        slope = slopes_ref[hp * G + g]
        bias = rel * (-slope)
        ids = (top_blocks(q, kmh_ref[:, heads[g]], kml_ref[:, heads[g]], n - 1)
               if n - 1 > MOBA_TOPK else None)
        peak = None
        for i in range(n):
            s = _dot_nt(qb, kbf_ref[i * T:(i + 1) * T, heads[g]]) * scale + bias
            far = -slope * float((n - 1 - i) * T)
            if i == n - 1:
                s = jnp.where(causal, s, NEG_BIG)
            elif ids is None:
                s = s + far
            else:
                chosen = functools.reduce(jnp.logical_or, [idt == float(i) for idt in ids])
                shift = jnp.where(chosen, far, NEG_BIG)
                s = s + jnp.concatenate([shift] * (T // LANES), axis=1)
            s_ref[g, :, i * T:(i + 1) * T] = s
            peak = s if peak is None else jnp.maximum(peak, s)
        m = jnp.max(peak, axis=-1, keepdims=True)
        mass = None
        acc = None
        for i in range(n):
            p = jnp.exp(s_ref[g, :, i * T:(i + 1) * T] - m)
            pv = _dot(p.astype(BF16), vbf_ref[i * T:(i + 1) * T, heads[g]])
            mass = p if mass is None else mass + p
            acc = pv if acc is None else acc + pv
        l = jnp.sum(mass, axis=-1, keepdims=True)
        o_ref[:, heads[g]] = (acc / l).astype(o_ref.dtype)

    for n in range(1, n_blocks + 1):
        @pl.when(qi == n - 1)
        def _(n=n):
            for g in range(G):
                attend(g, n)


def moba_prompt(z, seq_len, slopes, col0, cache_k, page_table):
    m = z.shape[0]
    n_seq = m // seq_len
    T = MOBA_BLOCK
    G = MOBA_HEADS_PER_STEP
    gw = G * HEAD_DIM
    assert seq_len % T == 0 and N_HEADS % G == 0 and col0 % G == 0
    nq = seq_len // T
    assert nq <= KM_ROWS
    c0 = col0 // G
    hg = N_HEADS // G
    n_sum_steps = page_table.size // (PAGE_SUM_BLOCKS_PER_STEP * PAGES_PER_BLOCK)
    ride_along = n_sum_steps == n_seq * hg * nq
    out_shape = [jax.ShapeDtypeStruct((m, N_HEADS * HEAD_DIM), BF16)]
    out_specs = [pl.BlockSpec((T, gw), lambda b, h, i, pt: (b * nq + i, h))]
    page_specs = []
    if ride_along:
        page_specs, sum_spec = _page_sum_specs(page_table, lambda b, h, i: (b * hg + h) * nq + i)
        out_shape.append(_key_sums_shape(page_table))
        out_specs.append(sum_spec)
    grid_spec = pltpu.PrefetchScalarGridSpec(
        num_scalar_prefetch=1,
        grid=(n_seq, hg, nq),
        in_specs=[pl.BlockSpec(memory_space=pltpu.SMEM),
                  pl.BlockSpec((T, gw), lambda b, h, i, pt: (b * nq + i, c0 + h)),
                  pl.BlockSpec((seq_len, gw), lambda b, h, i, pt: (b, c0 + hg + h)),
                  pl.BlockSpec((seq_len, gw), lambda b, h, i, pt: (b, c0 + 2 * hg + h))] + page_specs,
        out_specs=out_specs,
        scratch_shapes=[pltpu.VMEM((seq_len, gw), BF16), pltpu.VMEM((seq_len, gw), BF16),
                        pltpu.VMEM((KM_ROWS, gw), BF16), pltpu.VMEM((KM_ROWS, gw), BF16),
                        pltpu.VMEM((G, T, seq_len), F32)],
    )
    outs = pl.pallas_call(
        functools.partial(_moba_prompt_body, n_blocks=nq, n_page_refs=len(page_specs)),
        out_shape=out_shape,
        grid_spec=grid_spec,
        compiler_params=_params("parallel", "parallel", "arbitrary"),
        name="moba_prompt",
    )(page_table, slopes, z, z, z, *([cache_k] * len(page_specs)))
    if ride_along:
        return outs[0], outs[1]
    return outs[0], moba_block_key_sums(cache_k, page_table)


def _sample_topk_body(ks_ref, q_ref, idx_ref, ok_ref):
    n_blk = ks_ref.shape[1]
    km = ks_ref[0, :, 0, :] * (1.0 / MOBA_BLOCK)
    prod = km * q_ref[0]
    lane = lax.broadcasted_iota(jnp.int32, (n_blk, LANES), 1)
    blk = lax.broadcasted_iota(jnp.int32, (n_blk, LANES), 0)
    sc = jnp.full((n_blk, LANES), -jnp.inf, F32)
    for h in range(N_HEADS):
        col = jnp.sum(prod[:, h * HEAD_DIM:(h + 1) * HEAD_DIM], axis=-1, keepdims=True)
        sc = jnp.where(lane == h, col, sc)
    idxs, oks = [], []
    for _ in range(MOBA_TOPK):
        mx = jnp.max(sc, axis=0, keepdims=True)
        first = jnp.min(jnp.where(sc == mx, blk, n_blk), axis=0, keepdims=True)
        idxs.append(first)
        oks.append(jnp.logical_and(mx > -jnp.inf, mx < jnp.inf).astype(jnp.int32))
        sc = jnp.where(blk == first, -jnp.inf, sc)
    idx_ref[0] = jnp.concatenate(idxs, axis=0)
    ok_ref[0] = jnp.concatenate(oks, axis=0)


def moba_sample_topk(key_sums, q3):
    nb, n_blk, _, w = key_sums.shape
    assert n_blk >= MOBA_TOPK
    out = jax.ShapeDtypeStruct((nb, MOBA_TOPK, LANES), jnp.int32)
    ospec = pl.BlockSpec((1, MOBA_TOPK, LANES), lambda b: (b, 0, 0))
    return pl.pallas_call(
        _sample_topk_body,
        out_shape=(out, out),
        grid=(nb,),
        in_specs=[pl.BlockSpec((1, n_blk, 1, w), lambda b: (b, 0, 0, 0)),
                  pl.BlockSpec((1, 1, w), lambda b: (b, 0, 0))],
        out_specs=(ospec, ospec),
        compiler_params=_params("parallel"),
        name="moba_sample_topk",
    )(key_sums, q3)


def _sample_attn_body(pt_ref, idx_ref, ok_ref, q_ref, kn_ref, vn_ref, ck_ref, cv_ref, o_ref,
                      kbuf, vbuf, sem, *, past_len):
    b = pl.program_id(0)
    T = MOBA_BLOCK
    pages_per_blk = T // PAGE_ROWS
    n_sel = MOBA_TOPK
    scale = HEAD_DIM ** -0.5

    def copies(h):
        out = []
        for j in range(n_sel):
            blk = idx_ref[b, j * N_HEADS + h]
            for r in range(pages_per_blk):
                page = pt_ref[b, blk * pages_per_blk + r]
                dst = pl.ds((j * pages_per_blk + r) * PAGE_ROWS, PAGE_ROWS)
                out.append(pltpu.make_async_copy(ck_ref.at[page, :, h, :], kbuf.at[h, dst, :], sem.at[0, h]))
                out.append(pltpu.make_async_copy(cv_ref.at[page, :, h, :], vbuf.at[h, dst, :], sem.at[1, h]))
        return out

    for h in range(N_HEADS):
        for cp in copies(h):
            cp.start()

    q = q_ref[0]
    kn = kn_ref[0]
    vn = vn_ref[0]
    colid = lax.broadcasted_iota(jnp.int32, (SUBLANES, n_sel * T), 1)
    for h in range(N_HEADS):
        for cp in copies(h):
            cp.wait()
        sl = slice(h * HEAD_DIM, (h + 1) * HEAD_DIM)
        slope = 2.0 ** (-8.0 * (h + 1) / N_HEADS)
        qh = q[:, sl]
        q8 = jnp.broadcast_to(qh, (SUBLANES, HEAD_DIM)).astype(BF16)
        s = _dot_nt(q8, kbuf[h].astype(BF16)) * scale
        dist = jnp.zeros(s.shape, F32)
        valid = jnp.zeros(s.shape, jnp.bool_)
        for j in range(n_sel):
            in_j = jnp.logical_and(colid >= j * T, colid < (j + 1) * T)
            key_pos = idx_ref[b, j * N_HEADS + h] * T - j * T + colid
            dist = jnp.where(in_j, (past_len - key_pos).astype(F32), dist)
            valid = jnp.logical_or(valid, jnp.logical_and(in_j, ok_ref[b, j * N_HEADS + h] > 0))
        s = jnp.where(valid, s - slope * dist, NEG_BIG)
        s_new = jnp.sum(qh * kn[:, sl], axis=-1, keepdims=True) * scale
        m = jnp.maximum(jnp.max(s, axis=-1, keepdims=True), s_new)
        p = jnp.exp(s - m)
        p_new = jnp.exp(s_new - m)
        l = jnp.sum(p, axis=-1, keepdims=True) + p_new
        acc = _dot(p.astype(BF16), vbuf[h].astype(BF16)) + p_new * vn[:, sl]
        o_ref[0, :, sl] = (acc[0:1] / l[0:1]).astype(o_ref.dtype)


def moba_sample_attend(q3, z3, cache_k, cache_v, page_table, idx, ok, kcol, past_len):
    nb = z3.shape[0]
    w = N_HEADS * HEAD_DIM
    rows = MOBA_TOPK * MOBA_BLOCK
    zspec = lambda g: pl.BlockSpec((1, 1, w), lambda b, *_: (b, 0, kcol + g))
    grid_spec = pltpu.PrefetchScalarGridSpec(
        num_scalar_prefetch=3,
        grid=(nb,),
        in_specs=[pl.BlockSpec((1, 1, w), lambda b, *_: (b, 0, 0)), zspec(0), zspec(1),
                  pl.BlockSpec(memory_space=pl.ANY), pl.BlockSpec(memory_space=pl.ANY)],
        out_specs=pl.BlockSpec((1, 1, w), lambda b, *_: (b, 0, 0)),
        scratch_shapes=[pltpu.VMEM((N_HEADS, rows, HEAD_DIM), F32),
                        pltpu.VMEM((N_HEADS, rows, HEAD_DIM), F32),
                        pltpu.SemaphoreType.DMA((2, N_HEADS))],
    )
    out = pl.pallas_call(
        functools.partial(_sample_attn_body, past_len=past_len),
        out_shape=jax.ShapeDtypeStruct((nb, 1, w), BF16),
        grid_spec=grid_spec,
        compiler_params=_params("arbitrary"),
        name="moba_sample_attend",
    )(page_table, idx, ok, q3, z3, z3, cache_k, cache_v)
    return out.reshape(nb, w)


def _alibi_slopes():
    return 2.0 ** (-8.0 * jnp.arange(1, N_HEADS + 1, dtype=F32) / N_HEADS)


def kernel(x_prompt, x_sample, p_prompt, p_sample, cache_k, cache_v, page_table, state_hgrn,
           state_ffn_conv, lb_logits, w_in, b_f, g_onorm, w_out, g_n1, g_n2, w_up, w_conv, b_conv,
           w_down, w_ple, w_pg, g_pn, g_final):
    depth = w_in.shape[0]
    assert depth == 1
    B, L, D = x_prompt.shape
    Bd, Ld, _ = x_sample.shape
    assert Ld == 1
    hgw = N_HEADS * HEAD_DIM
    n_pages = page_table.shape[1]
    past_len = n_pages * PAGE_ROWS
    moba_col128 = 4 * N_HEADS
    moba_colw = 4

    lbs = jnp.cumsum(jax.nn.softmax(lb_logits.astype(F32), axis=0), axis=0)
    slopes = _alibi_slopes()
    i = 0
    lb = lbs[i]
    win, wout, wup, wdown, wpg, wple = (a[i].astype(BF16) for a in (w_in, w_out, w_up, w_down, w_pg, w_ple))
    wconv, bconv = pair_up_columns(w_conv[i]), pair_up_columns(b_conv[i])

    def tail(h0, attn_cat, pe, upconv):
        h1 = matmul_residual(attn_cat, wout, h0, name="out_proj")
        act, conv_new = upconv(rmsnorm_rows(h1, g_n2[i], BF16))
        h2 = matmul_residual(act, wdown, h1, tn=256, single_buffer_a=True, name="down_proj")
        h3 = matmul_gate(rmsnorm_rows(h2, g_pn[i], BF16), wpg, h2, pe.astype(BF16), wple)
        return rmsnorm_rows(h3, g_final, F32), conv_new

    xp = x_prompt.reshape(B * L, D)
    zp = matmul(rmsnorm_rows(xp, g_n1[i], BF16), win, name="in_proj")
    o_hg, s_prompt = hgrn2_prompt(zp, L, b_f[i], lb, g_onorm[i])
    o_mb, key_sums = moba_prompt(zp, L, slopes, moba_col128, cache_k[i], page_table)
    y_p, conv_p = tail(xp, jnp.concatenate([o_hg, o_mb], axis=-1), p_prompt[i].reshape(B * L, -1),
                       lambda xn: upconv_prompt(xn, wup, wconv, bconv, L))
    k_p = zp[:, 5 * hgw:6 * hgw].reshape(1, B, L, N_HEADS, HEAD_DIM)
    v_p = zp[:, 6 * hgw:7 * hgw].reshape(1, B, L, N_HEADS, HEAD_DIM)

    xs = x_sample.reshape(Bd, D)
    xs_n = rmsnorm_rows(xs, g_n1[i], F32)
    zs = matmul(xs_n.astype(BF16), win, name="in_proj_s")
    o_hg_s, s_sample = hgrn2_sample(zs, b_f[i], lb, g_onorm[i], state_hgrn[i])
    zs3 = zs.reshape(Bd, 1, -1)
    qs3 = matmul_f32_cols(xs_n, w_in[i], moba_colw * hgw, hgw, name="in_proj_s_q").reshape(Bd, 1, hgw)
    idx, ok = moba_sample_topk(key_sums, qs3)
    o_mb_s = moba_sample_attend(qs3, zs3, cache_k[i], cache_v[i], page_table,
                                idx[:, :, :N_HEADS].reshape(Bd, -1), ok[:, :, :N_HEADS].reshape(Bd, -1),
                                moba_colw + 1, past_len)
    y_s, conv_s = tail(xs, jnp.concatenate([o_hg_s, o_mb_s], axis=-1), p_sample[i].reshape(Bd, -1),
                       lambda xn: upconv_sample(xn, wup, wconv, bconv, state_ffn_conv[i]))
    k_s = zs[:, 5 * hgw:6 * hgw].reshape(1, Bd, 1, N_HEADS, HEAD_DIM)
    v_s = zs[:, 6 * hgw:7 * hgw].reshape(1, Bd, 1, N_HEADS, HEAD_DIM)

    return (y_p.reshape(B, L, D), y_s.reshape(Bd, 1, D), k_p, v_p, s_prompt[None], conv_p[None],
            k_s, v_s, s_sample[None], conv_s[None])
```

```python
import functools

import jax
import jax.numpy as jnp
from jax import lax
from jax.experimental import pallas as pl
from jax.experimental.pallas import tpu as pltpu

F32 = jnp.float32
BF16 = jnp.bfloat16

HEAD_DIM = 128
N_HEADS = 16
MOBA_BLOCK = 256
MOBA_TOPK = 3
GLA_CHUNK = 64
CONV_W = 3
PAGE_ROWS = 128
RMS_EPS = 1e-6
NEG_BIG = -1e30

V7X_VMEM_LIMIT_BYTES = 56 * 1024 * 1024
SUBLANES = 8
LANES = 128


def _params(*sem):
    return pltpu.CompilerParams(dimension_semantics=sem, vmem_limit_bytes=V7X_VMEM_LIMIT_BYTES)


def _dot(a, b):
    return jnp.dot(a, b, preferred_element_type=F32)


def _dot_nt(a, b):
    return lax.dot_general(a, b, (((1,), (1,)), ((), ())), preferred_element_type=F32)


def _dot_tn(a, b):
    return lax.dot_general(a, b, (((0,), (0,)), ((), ())), preferred_element_type=F32)


def _sigmoid_pair(x):
    e = jnp.exp(-jnp.abs(x))
    r = 1.0 / (1.0 + e)
    er = e * r
    pos = x >= 0
    return jnp.where(pos, r, er), jnp.where(pos, er, r)


def _silu(x):
    return x * _sigmoid_pair(x)[0]


def _rms_body(x_ref, g_ref, o_ref):
    x = x_ref[...]
    ms = jnp.mean(x * x, axis=-1, keepdims=True)
    o_ref[...] = (x * lax.rsqrt(ms + RMS_EPS) * g_ref[...]).astype(o_ref.dtype)


def rmsnorm_rows(x, g, out_dtype):
    m, d = x.shape
    tm = min(m, 256)
    return pl.pallas_call(
        _rms_body,
        out_shape=jax.ShapeDtypeStruct((m, d), out_dtype),
        grid=(m // tm,),
        in_specs=[pl.BlockSpec((tm, d), lambda i: (i, 0)), pl.BlockSpec((1, d), lambda i: (0, 0))],
        out_specs=pl.BlockSpec((tm, d), lambda i: (i, 0)),
        compiler_params=_params("parallel"),
        name="rmsnorm",
    )(x, g.reshape(1, d))


def _mm_plain_body(a_ref, w_ref, o_ref):
    o_ref[...] = _dot(a_ref[...], w_ref[...])


def _mm_res_body(a_ref, w_ref, r_ref, o_ref):
    o_ref[...] = r_ref[...] + _dot(a_ref[...], w_ref[...])


def _mm_gate_body(a_ref, w_ref, h_ref, pe_ref, wple_ref, o_ref):
    gate = _sigmoid_pair(_dot(a_ref[...], w_ref[...]))[0]
    o_ref[...] = h_ref[...] + gate * _dot(pe_ref[...], wple_ref[...])


def _mm_call(body, a, w, extra, extra_specs, tm, tn, single_buffer_a, name):
    m, k = a.shape
    n = w.shape[1]
    tm = min(tm, m)
    a_kw = dict(pipeline_mode=pl.Buffered(1)) if single_buffer_a else {}
    return pl.pallas_call(
        body,
        out_shape=jax.ShapeDtypeStruct((m, n), F32),
        grid=(m // tm, n // tn),
        in_specs=[pl.BlockSpec((tm, k), lambda i, j: (i, 0), **a_kw),
                  pl.BlockSpec((k, tn), lambda i, j: (0, j))] + extra_specs(tm, tn),
        out_specs=pl.BlockSpec((tm, tn), lambda i, j: (i, j)),
        compiler_params=_params("parallel", "arbitrary"),
        name=name,
    )(a, w, *extra)


def matmul(a, w, *, tm=1024, tn=1024, name="matmul"):
    return _mm_call(_mm_plain_body, a, w, (), lambda tm, tn: [], tm, tn, False, name)


def matmul_residual(a, w, res, *, tm=1024, tn=512, single_buffer_a=False, name="matmul_residual"):
    specs = lambda tm, tn: [pl.BlockSpec((tm, tn), lambda i, j: (i, j))]
    return _mm_call(_mm_res_body, a, w, (res,), specs, tm, tn, single_buffer_a, name)


def matmul_gate(a, w, h, pe, wple, *, tm=1024, tn=512, name="matmul_gate"):
    kp = pe.shape[1]
    specs = lambda tm, tn: [pl.BlockSpec((tm, tn), lambda i, j: (i, j)),
                            pl.BlockSpec((tm, kp), lambda i, j: (i, 0)),
                            pl.BlockSpec((kp, tn), lambda i, j: (0, j))]
    return _mm_call(_mm_gate_body, a, w, (h, pe, wple), specs, tm, tn, False, name)


def _split2_bf16(x):
    hi = x.astype(BF16)
    return hi, (x - hi.astype(F32)).astype(BF16)


def _mm_x3_body(a_ref, w_ref, o_ref):
    ah, al = _split2_bf16(a_ref[...])
    wh, wl = _split2_bf16(w_ref[...])
    o_ref[...] = (_dot(al, wh) + _dot(ah, wl)) + _dot(ah, wh)


def matmul_f32_cols(a, w, col0, n, *, tn=512, name="matmul_x3"):
    m, k = a.shape
    assert col0 % tn == 0 and n % tn == 0
    return pl.pallas_call(
        _mm_x3_body,
        out_shape=jax.ShapeDtypeStruct((m, n), F32),
        grid=(n // tn,),
        in_specs=[pl.BlockSpec((m, k), lambda j: (0, 0)),
                  pl.BlockSpec((k, tn), lambda j: (0, col0 // tn + j))],
        out_specs=pl.BlockSpec((m, tn), lambda j: (0, j)),
        compiler_params=_params("parallel"),
        name=name,
    )(a, w)


UP_TN = 256


def pair_up_columns(a, tn=UP_TN):
    lead, n = a.shape[:-1], a.shape[-1]
    nj = n // (2 * tn)
    return a.reshape(lead + (2, nj, tn)).swapaxes(-3, -2).reshape(lead + (n,))


def _paired_weight(wa_ref, wg_ref):
    return jnp.concatenate([wa_ref[...], wg_ref[...]], axis=1)


def _conv_gate(wc, b, um2, um1, u, tn):
    c = b + wc[0:1] * um2 + wc[1:2] * um1 + wc[2:3] * u
    return _silu(c[:, tn:]) * c[:, :tn]


def _upconv_prompt_body(x_ref, wa_ref, wg_ref, wc_ref, b_ref, act_ref, cna_ref, cng_ref, *u_refs):
    tm = x_ref.shape[0]
    tn = act_ref.shape[1]
    n_sub = len(u_refs)
    sub = tm // n_sub
    H = SUBLANES
    u_refs[0][0:H, :] = jnp.zeros((H, 2 * tn), F32)
    w = _paired_weight(wa_ref, wg_ref)
    wc = wc_ref[...]
    b = b_ref[...]

    def project(r):
        u = _dot(x_ref[r * sub:(r + 1) * sub, :], w)
        u_refs[r][H:, :] = u
        if r + 1 < n_sub:
            u_refs[r + 1][0:H, :] = u[sub - H:, :]
        else:
            cna_ref[0] = u[sub - (CONV_W - 1):, 0:tn]
            cng_ref[0] = u[sub - (CONV_W - 1):, tn:]

    def conv_gate(r):
        u_ref = u_refs[r]
        act = _conv_gate(wc, b, u_ref[H - 2:H - 2 + sub, :], u_ref[H - 1:H - 1 + sub, :],
                         u_ref[H:H + sub, :], tn)
        act_ref[r * sub:(r + 1) * sub, :] = act.astype(act_ref.dtype)

    project(0)
    for r in range(n_sub):
        if r + 1 < n_sub:
            project(r + 1)
        conv_gate(r)


def upconv_prompt(xn, w_up, w_conv_p, b_conv_p, seq_len, *, tn=UP_TN, sub=512):
    m, d = xn.shape
    dff = w_up.shape[1] // 2
    tm = seq_len
    sub = min(sub, tm)
    nj = dff // tn
    n_seq = m // seq_len
    cn = jax.ShapeDtypeStruct((n_seq, CONV_W - 1, dff), F32)
    cn_spec = pl.BlockSpec((1, CONV_W - 1, tn), lambda i, j: (i, 0, j))
    act, cna, cng = pl.pallas_call(
        _upconv_prompt_body,
        out_shape=(jax.ShapeDtypeStruct((m, dff), BF16), cn, cn),
        grid=(n_seq, nj),
        in_specs=[pl.BlockSpec((tm, d), lambda i, j: (i, 0), pipeline_mode=pl.Buffered(1)),
                  pl.BlockSpec((d, tn), lambda i, j: (0, j)),
                  pl.BlockSpec((d, tn), lambda i, j: (0, j + nj)),
                  pl.BlockSpec((CONV_W, 2 * tn), lambda i, j: (0, j)),
                  pl.BlockSpec((1, 2 * tn), lambda i, j: (0, j))],
        out_specs=(pl.BlockSpec((tm, tn), lambda i, j: (i, j)), cn_spec, cn_spec),
        scratch_shapes=[pltpu.VMEM((sub + SUBLANES, 2 * tn), F32)] * (tm // sub),
        compiler_params=_params("parallel", "arbitrary"),
        name="upconv_prompt",
    )(xn, w_up, w_up, w_conv_p, b_conv_p.reshape(1, 2 * dff))
    return act, jnp.concatenate([cna, cng], axis=-1)


def _upconv_sample_body(x_ref, wa_ref, wg_ref, wc_ref, b_ref, pa_ref, pg_ref, act_ref, cna_ref, cng_ref):
    tn = act_ref.shape[1]
    u = _dot(x_ref[...], _paired_weight(wa_ref, wg_ref))
    um2 = jnp.concatenate([pa_ref[:, 0, :], pg_ref[:, 0, :]], axis=-1)
    um1 = jnp.concatenate([pa_ref[:, 1, :], pg_ref[:, 1, :]], axis=-1)
    cna_ref[:, 0, :] = pa_ref[:, 1, :]
    cna_ref[:, 1, :] = u[:, :tn]
    cng_ref[:, 0, :] = pg_ref[:, 1, :]
    cng_ref[:, 1, :] = u[:, tn:]
    act_ref[...] = _conv_gate(wc_ref[...], b_ref[...], um2, um1, u, tn).astype(act_ref.dtype)


def upconv_sample(xn, w_up, w_conv_p, b_conv_p, conv_prev, *, tn=UP_TN):
    m, d = xn.shape
    dff = w_up.shape[1] // 2
    nj = dff // tn
    cn = jax.ShapeDtypeStruct((m, CONV_W - 1, dff), F32)
    cn_spec = pl.BlockSpec((m, CONV_W - 1, tn), lambda j: (0, 0, j))
    act, cna, cng = pl.pallas_call(
        _upconv_sample_body,
        out_shape=(jax.ShapeDtypeStruct((m, dff), BF16), cn, cn),
        grid=(nj,),
        in_specs=[pl.BlockSpec((m, d), lambda j: (0, 0)),
                  pl.BlockSpec((d, tn), lambda j: (0, j)),
                  pl.BlockSpec((d, tn), lambda j: (0, j + nj)),
                  pl.BlockSpec((CONV_W, 2 * tn), lambda j: (0, j)),
                  pl.BlockSpec((1, 2 * tn), lambda j: (0, j)),
                  pl.BlockSpec((m, CONV_W - 1, tn), lambda j: (0, 0, j)),
                  pl.BlockSpec((m, CONV_W - 1, tn), lambda j: (0, 0, j + nj))],
        out_specs=(pl.BlockSpec((m, tn), lambda j: (0, j)), cn_spec, cn_spec),
        compiler_params=_params("parallel"),
        name="upconv_sample",
    )(xn, w_up, w_up, w_conv_p, b_conv_p.reshape(1, 2 * dff), conv_prev, conv_prev)
    return act, jnp.concatenate([cna, cng], axis=-1)


def _split3_bf16(x):
    hi = x.astype(BF16)
    r1 = x - hi.astype(F32)
    mid = r1.astype(BF16)
    lo = (r1 - mid.astype(F32)).astype(BF16)
    return hi, mid, lo


def _gla_prompt_body(zq_ref, zf_ref, zi_ref, zg_ref, bf_ref, lb_ref, gon_ref, o_ref, s_ref, st_ref):
    c = pl.program_id(1)
    n_chunks = pl.num_programs(1)
    C = zq_ref.shape[0]

    @pl.when(c == 0)
    def _():
        st_ref[...] = jnp.zeros(st_ref.shape, F32)

    lb = lb_ref[...]
    sig, nsig = _sigmoid_pair(zf_ref[...] + bf_ref[...])
    logf = jnp.log(lb + (1.0 - lb) * sig)
    key = (1.0 - lb) * nsig

    row = lax.broadcasted_iota(jnp.int32, (C, C), 0)
    col = lax.broadcasted_iota(jnp.int32, (C, C), 1)
    causal = row >= col
    tri = causal.astype(BF16)
    hi, mid, lo = _split3_bf16(logf)
    cum = (_dot(tri, lo) + _dot(tri, mid)) + _dot(tri, hi)

    last = cum[C - 1:C, :]
    q_dec = zq_ref[...] * jnp.exp(cum)
    k_inv = key * jnp.exp(-cum)
    k_last = key * jnp.exp(last - cum)
    decay = jnp.exp(last)
    gon = gon_ref[...]

    for h in range(N_HEADS):
        sl = slice(h * HEAD_DIM, (h + 1) * HEAD_DIM)
        qh = q_dec[:, sl].astype(BF16)
        vh = zi_ref[:, sl].astype(BF16)
        att = jnp.where(causal, _dot_nt(qh, k_inv[:, sl].astype(BF16)), 0.0)
        st = st_ref[h]
        o = _dot_nt(qh, st.astype(BF16)) + _dot(att.astype(BF16), vh)
        st_ref[h] = decay[:, sl] * st + _dot_tn(vh, k_last[:, sl].astype(BF16))
        ms = jnp.mean(o * o, axis=-1, keepdims=True)
        on = o * lax.rsqrt(ms + RMS_EPS) * gon
        o_ref[:, sl] = (on * _silu(zg_ref[:, sl])).astype(o_ref.dtype)

    @pl.when(c == n_chunks - 1)
    def _():
        for h in range(N_HEADS):
            s_ref[0, h] = st_ref[h].T


def hgrn2_prompt(z, seq_len, b_f, lb, g_onorm):
    m = z.shape[0]
    w = N_HEADS * HEAD_DIM
    n_seq = m // seq_len
    C = GLA_CHUNK if seq_len % GLA_CHUNK == 0 else seq_len
    nc = seq_len // C
    zspec = lambda g: pl.BlockSpec((C, w), lambda b, c: (b * nc + c, g))
    vec = pl.BlockSpec((1, w), lambda b, c: (0, 0))
    return pl.pallas_call(
        _gla_prompt_body,
        out_shape=(jax.ShapeDtypeStruct((m, w), BF16),
                   jax.ShapeDtypeStruct((n_seq, N_HEADS, HEAD_DIM, HEAD_DIM), F32)),
        grid=(n_seq, nc),
        in_specs=[zspec(0), zspec(1), zspec(2), zspec(3), vec, vec,
                  pl.BlockSpec((1, HEAD_DIM), lambda b, c: (0, 0))],
        out_specs=(pl.BlockSpec((C, w), lambda b, c: (b * nc + c, 0)),
                   pl.BlockSpec((1, N_HEADS, HEAD_DIM, HEAD_DIM), lambda b, c: (b, 0, 0, 0))),
        scratch_shapes=[pltpu.VMEM((N_HEADS, HEAD_DIM, HEAD_DIM), F32)],
        compiler_params=_params("parallel", "arbitrary"),
        name="hgrn2_prompt",
    )(z, z, z, z, b_f.reshape(1, w), lb.reshape(1, w), g_onorm.reshape(1, HEAD_DIM))


def _hgrn_sample_body(zqT_ref, zfT_ref, zi_ref, zg_ref, bfT_ref, lbT_ref, gon_ref, s_ref,
                      o_ref, so_ref):
    nb = zi_ref.shape[0]
    lb = lbT_ref[...]
    sig, nsig = _sigmoid_pair(zfT_ref[...] + bfT_ref[...])
    f = lb + (1.0 - lb) * sig
    key = (1.0 - lb) * nsig
    q = zqT_ref[...]
    v = zi_ref[...]
    rows = []
    for b in range(nb):
        s_new = f[:, b:b + 1] * s_ref[b, 0] + key[:, b:b + 1] * v[b:b + 1, :]
        so_ref[b, 0] = s_new
        rows.append(jnp.sum(q[:, b:b + 1] * s_new, axis=0, keepdims=True))
    o = jnp.concatenate(rows, axis=0)
    ms = jnp.mean(o * o, axis=-1, keepdims=True)
    on = o * lax.rsqrt(ms + RMS_EPS) * gon_ref[...]
    o_ref[...] = (on * _silu(zg_ref[...])).astype(o_ref.dtype)


def hgrn2_sample(z, b_f, lb, g_onorm, s0):
    nb = z.shape[0]
    w = N_HEADS * HEAD_DIM
    zt = z[:, :2 * w].T
    colspec = lambda g: pl.BlockSpec((HEAD_DIM, nb), lambda h: (g * N_HEADS + h, 0))
    rowspec = lambda g: pl.BlockSpec((nb, HEAD_DIM), lambda h: (0, g * N_HEADS + h))
    vecT = pl.BlockSpec((HEAD_DIM, 1), lambda h: (h, 0))
    sspec = pl.BlockSpec((nb, 1, HEAD_DIM, HEAD_DIM), lambda h: (0, h, 0, 0))
    return pl.pallas_call(
        _hgrn_sample_body,
        out_shape=(jax.ShapeDtypeStruct((nb, w), BF16), jax.ShapeDtypeStruct(s0.shape, F32)),
        grid=(N_HEADS,),
        in_specs=[colspec(0), colspec(1), rowspec(2), rowspec(3), vecT, vecT,
                  pl.BlockSpec((1, HEAD_DIM), lambda h: (0, 0)), sspec],
        out_specs=(pl.BlockSpec((nb, HEAD_DIM), lambda h: (0, h)), sspec),
        compiler_params=_params("parallel"),
        name="hgrn2_sample",
    )(zt, zt, z, z, b_f.reshape(w, 1), lb.reshape(w, 1), g_onorm.reshape(1, HEAD_DIM), s0)


PAGE_SUM_BLOCKS_PER_STEP = 4
PAGES_PER_BLOCK = MOBA_BLOCK // PAGE_ROWS


def _page_sums(page_refs, o_ref):
    for g in range(len(page_refs) // PAGES_PER_BLOCK):
        s = page_refs[g * PAGES_PER_BLOCK][0].sum(axis=0)
        for r in range(1, PAGES_PER_BLOCK):
            s = s + page_refs[g * PAGES_PER_BLOCK + r][0].sum(axis=0)
        for h in range(N_HEADS):
            o_ref[0, g, :, h * HEAD_DIM:(h + 1) * HEAD_DIM] = s[h:h + 1, :]


def _page_sum_specs(page_table, step_of):
    n_pages = page_table.shape[1]
    P = PAGE_SUM_BLOCKS_PER_STEP * PAGES_PER_BLOCK
    steps_per_seq = n_pages // P
    assert n_pages % P == 0

    def page_spec(p):
        def index(*args):
            ids, pt = args[:-1], args[-1]
            s = step_of(*ids)
            return (pt[s // steps_per_seq, (s % steps_per_seq) * P + p], 0, 0, 0)
        return pl.BlockSpec((1, PAGE_ROWS, N_HEADS, HEAD_DIM), index)

    def out_index(*args):
        s = step_of(*args[:-1])
        return (s // steps_per_seq, s % steps_per_seq, 0, 0)

    out_spec = pl.BlockSpec((1, PAGE_SUM_BLOCKS_PER_STEP, 1, N_HEADS * HEAD_DIM), out_index)
    return [page_spec(p) for p in range(P)], out_spec


def _key_sums_shape(page_table):
    nb, n_pages = page_table.shape
    return jax.ShapeDtypeStruct((nb, n_pages // PAGES_PER_BLOCK, 1, N_HEADS * HEAD_DIM), F32)


def _page_sum_body(pt_ref, *refs):
    _page_sums(refs[:-1], refs[-1])


def moba_block_key_sums(cache_k, page_table):
    nb, n_pages = page_table.shape
    steps_per_seq = n_pages // (PAGE_SUM_BLOCKS_PER_STEP * PAGES_PER_BLOCK)
    in_specs, out_spec = _page_sum_specs(page_table, lambda b, j: b * steps_per_seq + j)
    grid_spec = pltpu.PrefetchScalarGridSpec(
        num_scalar_prefetch=1, grid=(nb, steps_per_seq), in_specs=in_specs, out_specs=out_spec)
    return pl.pallas_call(
        _page_sum_body,
        out_shape=_key_sums_shape(page_table),
        grid_spec=grid_spec,
        compiler_params=_params("parallel", "parallel"),
        name="moba_page_sums",
    )(page_table, *([cache_k] * len(in_specs)))


MOBA_HEADS_PER_STEP = 2
KM_ROWS = 16


def _moba_prompt_body(pt_ref, slopes_ref, q_ref, k_ref, v_ref, *rest, n_blocks, n_page_refs):
    page_refs, rest = rest[:n_page_refs], rest[n_page_refs:]
    if n_page_refs:
        o_ref, ks_ref, kbf_ref, vbf_ref, kmh_ref, kml_ref, s_ref = rest
        _page_sums(page_refs, ks_ref)
    else:
        o_ref, kbf_ref, vbf_ref, kmh_ref, kml_ref, s_ref = rest
    hp = pl.program_id(1)
    qi = pl.program_id(2)
    T = MOBA_BLOCK
    G = MOBA_HEADS_PER_STEP
    scale = HEAD_DIM ** -0.5
    heads = [slice(g * HEAD_DIM, (g + 1) * HEAD_DIM) for g in range(G)]

    @pl.when(qi == 0)
    def _():
        k = k_ref[...]
        kbf_ref[...] = k.astype(BF16)
        vbf_ref[...] = v_ref[...].astype(BF16)
        sums = [jnp.sum(k[b * T:(b + 1) * T], axis=0, keepdims=True) for b in range(n_blocks)]
        sums.append(jnp.zeros((KM_ROWS - n_blocks, G * HEAD_DIM), F32))
        km = jnp.concatenate(sums, axis=0) * (1.0 / T)
        hi, lo = _split2_bf16(km)
        kmh_ref[...] = hi
        kml_ref[...] = lo

    blk_id = lax.broadcasted_iota(jnp.int32, (KM_ROWS, T), 0)

    def top_blocks(q, kmh, kml, n_past):
        qh, ql = _split2_bf16(q)
        sc = (_dot_nt(kmh, ql) + _dot_nt(kml, qh)) + _dot_nt(kmh, qh)
        sc = jnp.where(blk_id < n_past, sc, -jnp.inf)
        picked = jnp.zeros(sc.shape, jnp.bool_)
        for _ in range(MOBA_TOPK):
            mx = jnp.max(sc, axis=0, keepdims=True)
            hit = jnp.logical_and(sc == mx, mx > -jnp.inf)
            first = jnp.min(jnp.where(hit, blk_id, KM_ROWS), axis=0, keepdims=True)
            pick = blk_id == first
            picked = jnp.logical_or(picked, pick)
            sc = jnp.where(pick, -jnp.inf, sc)
        onehot = picked.astype(F32).astype(BF16)
        row_id = lax.broadcasted_iota(jnp.int32, (KM_ROWS, LANES), 0)
        return [_dot_tn(onehot, (row_id == i).astype(BF16)) for i in range(n_past)]

    r = lax.broadcasted_iota(jnp.int32, (T, T), 0)
    cidx = lax.broadcasted_iota(jnp.int32, (T, T), 1)
    causal = r >= cidx
    rel = (r - cidx).astype(F32)

    def attend(g, n):
        q = q_ref[:, heads[g]]
        qb = q.astype(BF16)
        slope = slopes_ref[hp * G + g]
        bias = rel * (-slope)
        sel = (top_blocks(q, kmh_ref[:, heads[g]], kml_ref[:, heads[g]], n - 1)
               if n - 1 > MOBA_TOPK else None)
        peak = None
        for i in range(n):
            s = _dot_nt(qb, kbf_ref[i * T:(i + 1) * T, heads[g]]) * scale + bias
            far = -slope * float((n - 1 - i) * T)
            if i == n - 1:
                s = jnp.where(causal, s, NEG_BIG)
            elif sel is None:
                s = s + far
            else:
                shift = jnp.where(sel[i] > 0.5, far, NEG_BIG)
                s = s + jnp.concatenate([shift] * (T // LANES), axis=1)
            s_ref[g, :, i * T:(i + 1) * T] = s
            peak = s if peak is None else jnp.maximum(peak, s)
        m = jnp.max(peak, axis=-1, keepdims=True)
        mass = None
        acc = None
        for i in range(n):
            p = jnp.exp(s_ref[g, :, i * T:(i + 1) * T] - m)
            pv = _dot(p.astype(BF16), vbf_ref[i * T:(i + 1) * T, heads[g]])
            mass = p if mass is None else mass + p
            acc = pv if acc is None else acc + pv
        l = jnp.sum(mass, axis=-1, keepdims=True)
        o_ref[:, heads[g]] = (acc / l).astype(o_ref.dtype)

    for n in range(1, n_blocks + 1):
        @pl.when(qi == n - 1)
        def _(n=n):
            for g in range(G):
                attend(g, n)


def moba_prompt(z, seq_len, slopes, col0, cache_k, page_table):
    m = z.shape[0]
    n_seq = m // seq_len
    T = MOBA_BLOCK
    G = MOBA_HEADS_PER_STEP
    gw = G * HEAD_DIM
    assert seq_len % T == 0 and N_HEADS % G == 0 and col0 % G == 0
    nq = seq_len // T
    assert nq <= KM_ROWS
    c0 = col0 // G
    hg = N_HEADS // G
    n_sum_steps = page_table.size // (PAGE_SUM_BLOCKS_PER_STEP * PAGES_PER_BLOCK)
    ride_along = n_sum_steps == n_seq * hg * nq
    out_shape = [jax.ShapeDtypeStruct((m, N_HEADS * HEAD_DIM), BF16)]
    out_specs = [pl.BlockSpec((T, gw), lambda b, h, i, pt: (b * nq + i, h))]
    page_specs = []
    if ride_along:
        page_specs, sum_spec = _page_sum_specs(page_table, lambda b, h, i: (b * hg + h) * nq + i)
        out_shape.append(_key_sums_shape(page_table))
        out_specs.append(sum_spec)
    grid_spec = pltpu.PrefetchScalarGridSpec(
        num_scalar_prefetch=1,
        grid=(n_seq, hg, nq),
        in_specs=[pl.BlockSpec(memory_space=pltpu.SMEM),
                  pl.BlockSpec((T, gw), lambda b, h, i, pt: (b * nq + i, c0 + h)),
                  pl.BlockSpec((seq_len, gw), lambda b, h, i, pt: (b, c0 + hg + h)),
                  pl.BlockSpec((seq_len, gw), lambda b, h, i, pt: (b, c0 + 2 * hg + h))] + page_specs,
        out_specs=out_specs,
        scratch_shapes=[pltpu.VMEM((seq_len, gw), BF16), pltpu.VMEM((seq_len, gw), BF16),
                        pltpu.VMEM((KM_ROWS, gw), BF16), pltpu.VMEM((KM_ROWS, gw), BF16),
                        pltpu.VMEM((G, T, seq_len), F32)],
    )
    outs = pl.pallas_call(
        functools.partial(_moba_prompt_body, n_blocks=nq, n_page_refs=len(page_specs)),
        out_shape=out_shape,
        grid_spec=grid_spec,
        compiler_params=_params("parallel", "parallel", "arbitrary"),
        name="moba_prompt",
    )(page_table, slopes, z, z, z, *([cache_k] * len(page_specs)))
    if ride_along:
        return outs[0], outs[1]
    return outs[0], moba_block_key_sums(cache_k, page_table)


def _sample_topk_body(ks_ref, q_ref, idx_ref, ok_ref):
    n_blk = ks_ref.shape[1]
    km = ks_ref[0, :, 0, :] * (1.0 / MOBA_BLOCK)
    prod = km * q_ref[0]
    lane = lax.broadcasted_iota(jnp.int32, (n_blk, LANES), 1)
    blk = lax.broadcasted_iota(jnp.int32, (n_blk, LANES), 0)
    sc = jnp.full((n_blk, LANES), -jnp.inf, F32)
    for h in range(N_HEADS):
        col = jnp.sum(prod[:, h * HEAD_DIM:(h + 1) * HEAD_DIM], axis=-1, keepdims=True)
        sc = jnp.where(lane == h, col, sc)
    idxs, oks = [], []
    for _ in range(MOBA_TOPK):
        mx = jnp.max(sc, axis=0, keepdims=True)
        first = jnp.min(jnp.where(sc == mx, blk, n_blk), axis=0, keepdims=True)
        idxs.append(first)
        oks.append(jnp.logical_and(mx > -jnp.inf, mx < jnp.inf).astype(jnp.int32))
        sc = jnp.where(blk == first, -jnp.inf, sc)
    idx_ref[0] = jnp.concatenate(idxs, axis=0)
    ok_ref[0] = jnp.concatenate(oks, axis=0)


def moba_sample_topk(key_sums, q3):
    nb, n_blk, _, w = key_sums.shape
    assert n_blk >= MOBA_TOPK
    out = jax.ShapeDtypeStruct((nb, MOBA_TOPK, LANES), jnp.int32)
    ospec = pl.BlockSpec((1, MOBA_TOPK, LANES), lambda b: (b, 0, 0))
    return pl.pallas_call(
        _sample_topk_body,
        out_shape=(out, out),
        grid=(nb,),
        in_specs=[pl.BlockSpec((1, n_blk, 1, w), lambda b: (b, 0, 0, 0)),
                  pl.BlockSpec((1, 1, w), lambda b: (b, 0, 0))],
        out_specs=(ospec, ospec),
        compiler_params=_params("parallel"),
        name="moba_sample_topk",
    )(key_sums, q3)


def _sample_attn_body(pt_ref, idx_ref, ok_ref, q_ref, kn_ref, vn_ref, ck_ref, cv_ref, o_ref,
                      kbuf, vbuf, sem, *, past_len):
    b = pl.program_id(0)
    T = MOBA_BLOCK
    pages_per_blk = T // PAGE_ROWS
    n_sel = MOBA_TOPK
    scale = HEAD_DIM ** -0.5

    def copies(h):
        out = []
        for j in range(n_sel):
            blk = idx_ref[b, j * N_HEADS + h]
            for r in range(pages_per_blk):
                page = pt_ref[b, blk * pages_per_blk + r]
                dst = pl.ds((j * pages_per_blk + r) * PAGE_ROWS, PAGE_ROWS)
                out.append(pltpu.make_async_copy(ck_ref.at[page, :, h, :], kbuf.at[h, dst, :], sem.at[0, h]))
                out.append(pltpu.make_async_copy(cv_ref.at[page, :, h, :], vbuf.at[h, dst, :], sem.at[1, h]))
        return out

    for h in range(N_HEADS):
        for cp in copies(h):
            cp.start()

    q = q_ref[0]
    kn = kn_ref[0]
    vn = vn_ref[0]
    colid = lax.broadcasted_iota(jnp.int32, (SUBLANES, n_sel * T), 1)
    for h in range(N_HEADS):
        for cp in copies(h):
            cp.wait()
    for h in range(N_HEADS):
        sl = slice(h * HEAD_DIM, (h + 1) * HEAD_DIM)
        slope = 2.0 ** (-8.0 * (h + 1) / N_HEADS)
        qh = q[:, sl]
        q8 = jnp.broadcast_to(qh, (SUBLANES, HEAD_DIM)).astype(BF16)
        s = _dot_nt(q8, kbuf[h].astype(BF16)) * scale
        dist = jnp.zeros(s.shape, F32)
        valid = jnp.zeros(s.shape, jnp.bool_)
        for j in range(n_sel):
            in_j = jnp.logical_and(colid >= j * T, colid < (j + 1) * T)
            key_pos = idx_ref[b, j * N_HEADS + h] * T - j * T + colid
            dist = jnp.where(in_j, (past_len - key_pos).astype(F32), dist)
            valid = jnp.logical_or(valid, jnp.logical_and(in_j, ok_ref[b, j * N_HEADS + h] > 0))
        s = jnp.where(valid, s - slope * dist, NEG_BIG)
        s_new = jnp.sum(qh * kn[:, sl], axis=-1, keepdims=True) * scale
        m = jnp.maximum(jnp.max(s, axis=-1, keepdims=True), s_new)
        p = jnp.exp(s - m)
        p_new = jnp.exp(s_new - m)
        l = jnp.sum(p, axis=-1, keepdims=True) + p_new
        acc = _dot(p.astype(BF16), vbuf[h].astype(BF16)) + p_new * vn[:, sl]
        o_ref[0, :, sl] = (acc[0:1] / l[0:1]).astype(o_ref.dtype)


def moba_sample_attend(q3, z3, cache_k, cache_v, page_table, idx, ok, kcol, past_len):
    nb = z3.shape[0]
    w = N_HEADS * HEAD_DIM
    rows = MOBA_TOPK * MOBA_BLOCK
    zspec = lambda g: pl.BlockSpec((1, 1, w), lambda b, *_: (b, 0, kcol + g))
    grid_spec = pltpu.PrefetchScalarGridSpec(
        num_scalar_prefetch=3,
        grid=(nb,),
        in_specs=[pl.BlockSpec((1, 1, w), lambda b, *_: (b, 0, 0)), zspec(0), zspec(1),
                  pl.BlockSpec(memory_space=pl.ANY), pl.BlockSpec(memory_space=pl.ANY)],
        out_specs=pl.BlockSpec((1, 1, w), lambda b, *_: (b, 0, 0)),
        scratch_shapes=[pltpu.VMEM((N_HEADS, rows, HEAD_DIM), F32),
                        pltpu.VMEM((N_HEADS, rows, HEAD_DIM), F32),
                        pltpu.SemaphoreType.DMA((2, N_HEADS))],
    )
    out = pl.pallas_call(
        functools.partial(_sample_attn_body, past_len=past_len),
        out_shape=jax.ShapeDtypeStruct((nb, 1, w), BF16),
        grid_spec=grid_spec,
        compiler_params=_params("arbitrary"),
        name="moba_sample_attend",
    )(page_table, idx, ok, q3, z3, z3, cache_k, cache_v)
    return out.reshape(nb, w)


def _alibi_slopes():
    return 2.0 ** (-8.0 * jnp.arange(1, N_HEADS + 1, dtype=F32) / N_HEADS)


def kernel(x_prompt, x_sample, p_prompt, p_sample, cache_k, cache_v, page_table, state_hgrn,
           state_ffn_conv, lb_logits, w_in, b_f, g_onorm, w_out, g_n1, g_n2, w_up, w_conv, b_conv,
           w_down, w_ple, w_pg, g_pn, g_final):
    depth = w_in.shape[0]
    assert depth == 1
    B, L, D = x_prompt.shape
    Bd, Ld, _ = x_sample.shape
    assert Ld == 1
    hgw = N_HEADS * HEAD_DIM
    n_pages = page_table.shape[1]
    past_len = n_pages * PAGE_ROWS
    moba_col128 = 4 * N_HEADS
    moba_colw = 4

    lbs = jnp.cumsum(jax.nn.softmax(lb_logits.astype(F32), axis=0), axis=0)
    slopes = _alibi_slopes()
    i = 0
    lb = lbs[i]
    win, wout, wup, wdown, wpg, wple = (a[i].astype(BF16) for a in (w_in, w_out, w_up, w_down, w_pg, w_ple))
    wconv, bconv = pair_up_columns(w_conv[i]), pair_up_columns(b_conv[i])

    def tail(h0, attn_cat, pe, upconv):
        h1 = matmul_residual(attn_cat, wout, h0, name="out_proj")
        act, conv_new = upconv(rmsnorm_rows(h1, g_n2[i], BF16))
        h2 = matmul_residual(act, wdown, h1, tn=256, single_buffer_a=True, name="down_proj")
        h3 = matmul_gate(rmsnorm_rows(h2, g_pn[i], BF16), wpg, h2, pe.astype(BF16), wple)
        return rmsnorm_rows(h3, g_final, F32), conv_new

    xp = x_prompt.reshape(B * L, D)
    zp = matmul(rmsnorm_rows(xp, g_n1[i], BF16), win, name="in_proj")
    o_hg, s_prompt = hgrn2_prompt(zp, L, b_f[i], lb, g_onorm[i])
    o_mb, key_sums = moba_prompt(zp, L, slopes, moba_col128, cache_k[i], page_table)
    y_p, conv_p = tail(xp, jnp.concatenate([o_hg, o_mb], axis=-1), p_prompt[i].reshape(B * L, -1),
                       lambda xn: upconv_prompt(xn, wup, wconv, bconv, L))
    k_p = zp[:, 5 * hgw:6 * hgw].reshape(1, B, L, N_HEADS, HEAD_DIM)
    v_p = zp[:, 6 * hgw:7 * hgw].reshape(1, B, L, N_HEADS, HEAD_DIM)

    xs = x_sample.reshape(Bd, D)
    xs_n = rmsnorm_rows(xs, g_n1[i], F32)
    zs = matmul(xs_n.astype(BF16), win, name="in_proj_s")
    o_hg_s, s_sample = hgrn2_sample(zs, b_f[i], lb, g_onorm[i], state_hgrn[i])
    zs3 = zs.reshape(Bd, 1, -1)
    qs3 = matmul_f32_cols(xs_n, w_in[i], moba_colw * hgw, hgw, name="in_proj_s_q").reshape(Bd, 1, hgw)
    idx, ok = moba_sample_topk(key_sums, qs3)
    o_mb_s = moba_sample_attend(qs3, zs3, cache_k[i], cache_v[i], page_table,
                                idx[:, :, :N_HEADS].reshape(Bd, -1), ok[:, :, :N_HEADS].reshape(Bd, -1),
                                moba_colw + 1, past_len)
    y_s, conv_s = tail(xs, jnp.concatenate([o_hg_s, o_mb_s], axis=-1), p_sample[i].reshape(Bd, -1),
                       lambda xn: upconv_sample(xn, wup, wconv, bconv, state_ffn_conv[i]))
    k_s = zs[:, 5 * hgw:6 * hgw].reshape(1, Bd, 1, N_HEADS, HEAD_DIM)
    v_s = zs[:, 6 * hgw:7 * hgw].reshape(1, Bd, 1, N_HEADS, HEAD_DIM)

    return (y_p.reshape(B, L, D), y_s.reshape(Bd, 1, D), k_p, v_p, s_prompt[None], conv_p[None],
            k_s, v_s, s_sample[None], conv_s[None])
```
